```python
import jax, jax.numpy as jnp
from jax import lax
import numpy as np

D_MODEL = 2048
BATCH = 1
SEQ = 16384
DEPTH = 2

BRANCH_WIDTH = 512
N_BRANCHES = 4
GROUP_WIDTH = 128
N_GROUPS = BRANCH_WIDTH // GROUP_WIDTH
CONV_KERNEL = 31
POOL_WINDOWS = (2, 4, 8, 16)
SGU_CHUNK = 128
SHORT_CONV_KERNEL = 3
D_FF = 5632
N_SUBLAYERS = 3
N_MOD = 3
RMS_EPS = 1e-6
LN_EPS = 1e-5
MIX_SPLITS = (2 * BRANCH_WIDTH, 3 * BRANCH_WIDTH, 5 * BRANCH_WIDTH)
MIX_IN_WIDTH = 8 * BRANCH_WIDTH

kernel_name = "hybrid_gated_conv_pool_sgu_block"


def rms_norm(x, g):
    x32 = x.astype(jnp.float32)
    y = x32 * lax.rsqrt(jnp.mean(x32 * x32, axis=-1, keepdims=True) + RMS_EPS)
    return (y * g.astype(jnp.float32)).astype(x.dtype)


def layer_norm(x, g, b):
    x32 = x.astype(jnp.float32)
    mu = jnp.mean(x32, axis=-1, keepdims=True)
    xc = x32 - mu
    y = xc * lax.rsqrt(jnp.mean(xc * xc, axis=-1, keepdims=True) + LN_EPS)
    return (y * g.astype(jnp.float32) + b.astype(jnp.float32)).astype(x.dtype)


def modulate(n, shift, scale):
    return n * (1 + scale[:, None, :]) + shift[:, None, :]


def causal_depthwise_conv(x, w):
    k = w.shape[0]
    return lax.conv_general_dilated(
        x, w[:, None, :].astype(x.dtype), window_strides=(1,), padding=[(k - 1, 0)],
        dimension_numbers=("NWC", "WIO", "NWC"), feature_group_count=x.shape[-1])


def swiglu_ffn(n, w_in, w_out):
    gate, up = jnp.split(n @ w_in, 2, axis=-1)
    return (jax.nn.silu(gate) * up) @ w_out


def conformer_conv_mixer(p, conv_w, conv_b, ln_g, ln_b):
    val, gate = jnp.split(p, 2, axis=-1)
    a = val * jax.nn.sigmoid(gate)
    a = causal_depthwise_conv(a, conv_w) + conv_b
    return jax.nn.silu(layer_norm(a, ln_g, ln_b))


def multiscale_pool_mixer(p, group_w, scale):
    b, s, _ = p.shape
    xg = p.reshape(b, s, N_GROUPS, GROUP_WIDTH).astype(jnp.float32)
    pos1 = jnp.arange(1, s + 1, dtype=jnp.int32)
    outs = []
    for gi, win in enumerate(POOL_WINDOWS):
        seg = xg[:, :, gi]
        cs = jnp.cumsum(seg, axis=1)
        lag = jnp.pad(cs, ((0, 0), (win, 0), (0, 0)))[:, :s]
        cnt = jnp.minimum(pos1, win).astype(jnp.float32)[None, :, None]
        outs.append((cs - lag) / cnt - seg)
    pooled = jnp.stack(outs, axis=2).astype(p.dtype)
    mixed = jnp.einsum("bsgc,gcd->bsgd", pooled, group_w)
    return mixed.reshape(b, s, BRANCH_WIDTH) * scale


def spatial_gating_mixer(p, ln_g, ln_b, w_s, b_s):
    b, s, _ = p.shape
    u, v = jnp.split(jax.nn.gelu(p, approximate=False), 2, axis=-1)
    v = layer_norm(v, ln_g, ln_b).reshape(b, s // SGU_CHUNK, SGU_CHUNK, N_GROUPS, GROUP_WIDTH)
    mask = jnp.tril(jnp.ones((SGU_CHUNK, SGU_CHUNK), dtype=bool))
    w_causal = jnp.where(mask[None], w_s, 0)
    sv = jnp.einsum("gtj,bnjgc->bntgc", w_causal, v) + b_s.T[None, None, :, :, None]
    return u * sv.reshape(b, s, BRANCH_WIDTH)


def short_conv_mixer(p, conv_w):
    gb, gc, h = jnp.split(p, 3, axis=-1)
    return gb * causal_depthwise_conv(gc * h, conv_w)


def setup_inputs(seed: int = 0) -> dict:
    key = jax.random.key(seed)
    ks = jax.random.split(key, 24)
    L, D, W, F, G, T = DEPTH, D_MODEL, BRANCH_WIDTH, D_FF, N_GROUPS, SGU_CHUNK
    nrm = lambda k, shape, s: jax.random.normal(k, shape, jnp.float32) * s
    return {
        "x": nrm(ks[0], (BATCH, SEQ, D), 1.0),
        "c": nrm(ks[1], (BATCH, D), 1.0),
        "ada_w": nrm(ks[2], (L, D, N_SUBLAYERS * N_MOD * D), 0.5 * D ** -0.5),
        "ada_b": nrm(ks[3], (L, N_SUBLAYERS * N_MOD * D), 0.01),
        "pre_g": 1.0 + nrm(ks[4], (L, N_SUBLAYERS, D), 0.05),
        "post_g": 1.0 + nrm(ks[5], (L, N_SUBLAYERS, D), 0.05),
        "ffn_w_in": nrm(ks[6], (L, 2, D, 2 * F), D ** -0.5),
        "ffn_w_out": nrm(ks[7], (L, 2, F, D), F ** -0.5),
        "mix_w_in": nrm(ks[8], (L, D, MIX_IN_WIDTH), D ** -0.5),
        "gate_w": nrm(ks[9], (L, N_BRANCHES, D, D), D ** -0.5),
        "gate_b": nrm(ks[10], (L, N_BRANCHES, D), 0.01),
        "conv_w": nrm(ks[11], (L, CONV_KERNEL, W), CONV_KERNEL ** -0.5),
        "conv_b": nrm(ks[12], (L, W), 0.01),
        "conv_ln_g": 1.0 + nrm(ks[13], (L, W), 0.05),
        "conv_ln_b": nrm(ks[14], (L, W), 0.01),
        "pool_group_w": nrm(ks[15], (L, G, GROUP_WIDTH, GROUP_WIDTH), GROUP_WIDTH ** -0.5),
        "pool_scale": 1.0 + nrm(ks[16], (L, W), 0.05),
        "sgu_ln_g": 1.0 + nrm(ks[17], (L, W), 0.05),
        "sgu_ln_b": nrm(ks[18], (L, W), 0.01),
        "sgu_w_s": nrm(ks[19], (L, G, T, T), T ** -0.5),
        "sgu_b_s": 1.0 + nrm(ks[20], (L, G, T), 0.05),
        "sconv_w": nrm(ks[21], (L, SHORT_CONV_KERNEL, W), SHORT_CONV_KERNEL ** -0.5),
        "branch_w_out": nrm(ks[22], (L, N_BRANCHES, W, D), W ** -0.5),
        "w_o": nrm(ks[23], (L, D, D), D ** -0.5),
    }


def reference(x, c, ada_w, ada_b, pre_g, post_g, ffn_w_in, ffn_w_out, mix_w_in, gate_w, gate_b,
              conv_w, conv_b, conv_ln_g, conv_ln_b, pool_group_w, pool_scale, sgu_ln_g, sgu_ln_b,
              sgu_w_s, sgu_b_s, sconv_w, branch_w_out, w_o):
    b = x.shape[0]
    cond = jax.nn.silu(c)
    for l in range(DEPTH):
        ada = (cond @ ada_w[l] + ada_b[l]).reshape(b, N_SUBLAYERS, N_MOD, D_MODEL)

        n = modulate(rms_norm(x, pre_g[l, 0]), ada[:, 0, 0], ada[:, 0, 1])
        y = rms_norm(swiglu_ffn(n, ffn_w_in[l, 0], ffn_w_out[l, 0]), post_g[l, 0])
        x = x + 0.5 * ada[:, 0, 2][:, None, :] * y

        n = modulate(rms_norm(x, pre_g[l, 1]), ada[:, 1, 0], ada[:, 1, 1])
        p_conv, p_pool, p_sgu, p_sconv = jnp.split(n @ mix_w_in[l], MIX_SPLITS, axis=-1)
        y_conv = conformer_conv_mixer(p_conv, conv_w[l], conv_b[l], conv_ln_g[l], conv_ln_b[l])
        y_pool = multiscale_pool_mixer(p_pool, pool_group_w[l], pool_scale[l])
        y_sgu = spatial_gating_mixer(p_sgu, sgu_ln_g[l], sgu_ln_b[l], sgu_w_s[l], sgu_b_s[l])
        y_sconv = short_conv_mixer(p_sconv, sconv_w[l])
        merged = (jax.nn.sigmoid(n @ gate_w[l, 0] + gate_b[l, 0]) * (y_conv @ branch_w_out[l, 0])
                  + jax.nn.sigmoid(n @ gate_w[l, 1] + gate_b[l, 1]) * (y_pool @ branch_w_out[l, 1])
                  + jax.nn.sigmoid(n @ gate_w[l, 2] + gate_b[l, 2]) * (y_sgu @ branch_w_out[l, 2])
                  + jax.nn.sigmoid(n @ gate_w[l, 3] + gate_b[l, 3]) * (y_sconv @ branch_w_out[l, 3]))
        y = rms_norm(merged @ w_o[l], post_g[l, 1])
        x = x + ada[:, 1, 2][:, None, :] * y

        n = modulate(rms_norm(x, pre_g[l, 2]), ada[:, 2, 0], ada[:, 2, 1])
        y = rms_norm(swiglu_ffn(n, ffn_w_in[l, 1], ffn_w_out[l, 1]), post_g[l, 2])
        x = x + 0.5 * ada[:, 2, 2][:, None, :] * y
    return x
```

```python
import functools

import jax
import jax.numpy as jnp
from jax import lax
from jax.experimental import pallas as pl
from jax.experimental.pallas import tpu as pltpu

D_MODEL = 2048
SEQ = 16384
DEPTH = 2
BRANCH_WIDTH = 512
N_BRANCHES = 4
GROUP_WIDTH = 128
N_GROUPS = 4
CONV_KERNEL = 31
POOL_WINDOWS = (2, 4, 8, 16)
SGU_CHUNK = 128
SHORT_CONV_KERNEL = 3
D_FF = 5632
N_SUBLAYERS = 3
N_MOD = 3
ADA_WIDTH = N_SUBLAYERS * N_MOD * D_MODEL
MIX_IN_WIDTH = 8 * BRANCH_WIDTH
RMS_EPS = 1e-6
LN_EPS = 1e-5

V7X_VMEM_LIMIT_BYTES = 56 * 1024 * 1024
SUBLANES = 8

ADA_TN = 1024
FFN_TM = 1024
FFN_TF = 512
FFN_TN = 512
MIXA_TM = 512
MIXB_TM = 512
MIXB_TN = 256
CONV_HALO = 32
POOL_HALO = 16
SCONV_HALO = 8

_BF16 = jnp.bfloat16
_F32 = jnp.float32


def _sigmoid(v):
    return 1.0 / (1.0 + jnp.exp(-v))


def _silu(v):
    return v * _sigmoid(v)


def _rms_norm(v, g):
    return v * lax.rsqrt(jnp.mean(v * v, axis=-1, keepdims=True) + RMS_EPS) * g


def _layer_norm(v, g, b):
    mu = jnp.mean(v, axis=-1, keepdims=True)
    vc = v - mu
    return vc * lax.rsqrt(jnp.mean(vc * vc, axis=-1, keepdims=True) + LN_EPS) * g + b


def _mod(ada_ref, sub, which):
    off = (sub * N_MOD + which) * D_MODEL
    return ada_ref[0, :, off:off + D_MODEL]


def _modulated_norm(x, pre_g_ref, ada_ref, sub):
    n = _rms_norm(x, pre_g_ref[0, sub:sub + 1, :])
    return n * (1.0 + _mod(ada_ref, sub, 1)) + _mod(ada_ref, sub, 0)


def _ada_kernel(c_ref, w_ref, b_ref, o_ref):
    cond = _silu(c_ref[...])
    o_ref[0] = jnp.sum(w_ref[0] * cond, axis=0, keepdims=True) + b_ref[0]


def _ada_call(c_col, ada_w, ada_b3):
    return pl.pallas_call(
        _ada_kernel,
        grid=(DEPTH, ADA_WIDTH // ADA_TN),
        in_specs=[
            pl.BlockSpec((D_MODEL, 1), lambda l, j: (0, 0)),
            pl.BlockSpec((1, D_MODEL, ADA_TN), lambda l, j: (l, 0, j)),
            pl.BlockSpec((1, 1, ADA_TN), lambda l, j: (l, 0, j)),
        ],
        out_specs=pl.BlockSpec((1, 1, ADA_TN), lambda l, j: (l, 0, j)),
        out_shape=jax.ShapeDtypeStruct((DEPTH, 1, ADA_WIDTH), _F32),
        compiler_params=pltpu.CompilerParams(
            dimension_semantics=("arbitrary", "arbitrary"),
            vmem_limit_bytes=V7X_VMEM_LIMIT_BYTES),
        name="ada",
    )(c_col, ada_w, ada_b3)


def _ffn_kernel(sub, x_ref, ada_ref, pre_g_ref, post_g_ref, wg_ref, wu_ref, wo_ref,
                o_ref, n_scr):
    j = pl.program_id(1)

    @pl.when(j == 0)
    def _():
        n_scr[...] = _modulated_norm(x_ref[...], pre_g_ref, ada_ref, sub).astype(_BF16)
        o_ref[...] = jnp.zeros_like(o_ref)

    n = n_scr[...]
    hg = jnp.dot(n, wg_ref[...], preferred_element_type=_F32)
    hu = jnp.dot(n, wu_ref[...], preferred_element_type=_F32)
    a = (_silu(hg) * hu).astype(_BF16)
    for c in range(D_MODEL // FFN_TN):
        cols = slice(c * FFN_TN, (c + 1) * FFN_TN)
        o_ref[:, cols] += jnp.dot(a, wo_ref[:, cols], preferred_element_type=_F32)

    @pl.when(j == pl.num_programs(1) - 1)
    def _():
        y = _rms_norm(o_ref[...], post_g_ref[0, sub:sub + 1, :])
        o_ref[...] = x_ref[...] + (0.5 * _mod(ada_ref, sub, 2)) * y


def _ffn_call(x, ada, pre_g, post_g, w_in, w_out, layer, which, sub):
    n_f = D_FF // FFN_TF
    return pl.pallas_call(
        functools.partial(_ffn_kernel, sub),
        grid=(SEQ // FFN_TM, n_f),
        in_specs=[
            pl.BlockSpec((FFN_TM, D_MODEL), lambda i, j: (i, 0), pipeline_mode=pl.Buffered(1)),
            pl.BlockSpec((1, 1, ADA_WIDTH), lambda i, j: (layer, 0, 0)),
            pl.BlockSpec((1, N_SUBLAYERS, D_MODEL), lambda i, j: (layer, 0, 0)),
            pl.BlockSpec((1, N_SUBLAYERS, D_MODEL), lambda i, j: (layer, 0, 0)),
            pl.BlockSpec((None, None, D_MODEL, FFN_TF), lambda i, j: (layer, which, 0, j)),
            pl.BlockSpec((None, None, D_MODEL, FFN_TF), lambda i, j: (layer, which, 0, j + n_f)),
            pl.BlockSpec((None, None, FFN_TF, D_MODEL), lambda i, j: (layer, which, j, 0)),
        ],
        out_specs=pl.BlockSpec((FFN_TM, D_MODEL), lambda i, j: (i, 0)),
        out_shape=jax.ShapeDtypeStruct((SEQ, D_MODEL), _F32),
        scratch_shapes=[pltpu.VMEM((FFN_TM, D_MODEL), _BF16)],
        compiler_params=pltpu.CompilerParams(
            dimension_semantics=("arbitrary", "arbitrary"),
            vmem_limit_bytes=V7X_VMEM_LIMIT_BYTES),
        name=f"ffn_l{layer}_{which}",
    )(x, ada, pre_g, post_g, w_in, w_in, w_out)


def _shift_history(buf, halo, tm):
    buf[0:halo, :] = buf[tm:tm + halo, :]


def _mixa_kernel(x_ref, ada_ref, pre_g_ref, w_ref, conv_w_ref, conv_b_ref, cln_g_ref, cln_b_ref,
                 pool_w_ref, pool_s_ref, sln_g_ref, sln_b_ref, ws_ref, bs_ref, sconv_w_ref,
                 n_ref, y_ref, conv_buf, pool_buf, sconv_buf):
    tm = MIXA_TM
    W = BRANCH_WIDTH
    i = pl.program_id(0)

    @pl.when(i == 0)
    def _():
        conv_buf[0:CONV_HALO, :] = jnp.zeros((CONV_HALO, W), _F32)
        pool_buf[0:POOL_HALO, :] = jnp.zeros((POOL_HALO, W), _F32)
        sconv_buf[0:SCONV_HALO, :] = jnp.zeros((SCONV_HALO, W), _F32)

    n = _modulated_norm(x_ref[...], pre_g_ref, ada_ref, 1).astype(_BF16)
    n_ref[...] = n

    p = jnp.dot(n, w_ref[:, 0:2 * W], preferred_element_type=_F32)
    conv_buf[CONV_HALO:CONV_HALO + tm, :] = p[:, 0:W] * _sigmoid(p[:, W:2 * W])
    base = CONV_HALO - (CONV_KERNEL - 1)
    acc = conv_w_ref[0, 0:1, :] * conv_buf[base:base + tm, :]
    for k in range(1, CONV_KERNEL):
        acc = acc + conv_w_ref[0, k:k + 1, :] * conv_buf[base + k:base + k + tm, :]
    acc = acc + conv_b_ref[0]
    y_ref[:, 0:W] = _silu(_layer_norm(acc, cln_g_ref[0], cln_b_ref[0])).astype(_BF16)
    _shift_history(conv_buf, CONV_HALO, tm)

    p = jnp.dot(n, w_ref[:, 2 * W:3 * W], preferred_element_type=_F32)
    pool_buf[POOL_HALO:POOL_HALO + tm, :] = p
    pos1 = (lax.broadcasted_iota(jnp.int32, (tm, 1), 0) + (i * tm + 1)).astype(_F32)
    for g, win in enumerate(POOL_WINDOWS):
        lanes = slice(g * GROUP_WIDTH, (g + 1) * GROUP_WIDTH)
        tok = p[:, lanes]
        tot = tok
        for d in range(1, win):
            tot = tot + pool_buf[POOL_HALO - d:POOL_HALO - d + tm, lanes]
        pooled = tot / jnp.minimum(pos1, float(win)) - tok
        mixed = jnp.dot(pooled.astype(_BF16), pool_w_ref[0, g], preferred_element_type=_F32)
        off = W + g * GROUP_WIDTH
        y_ref[:, off:off + GROUP_WIDTH] = (mixed * pool_s_ref[0, :, lanes]).astype(_BF16)
    _shift_history(pool_buf, POOL_HALO, tm)

    p = jnp.dot(n, w_ref[:, 3 * W:5 * W], preferred_element_type=_F32)
    p = 0.5 * p * (1.0 + lax.erf(p * (2.0 ** -0.5)))
    u = p[:, 0:W]
    v = _layer_norm(p[:, W:2 * W], sln_g_ref[0], sln_b_ref[0]).astype(_BF16)
    n_chunks = tm // SGU_CHUNK
    row = lax.broadcasted_iota(jnp.int32, (SGU_CHUNK, SGU_CHUNK), 0)
    col = lax.broadcasted_iota(jnp.int32, (SGU_CHUNK, SGU_CHUNK), 1)
    for g in range(N_GROUPS):
        lanes = slice(g * GROUP_WIDTH, (g + 1) * GROUP_WIDTH)
        w_causal = jnp.where(col <= row, ws_ref[0, g], 0.0).astype(_BF16)
        vg = jnp.concatenate(
            [v[c * SGU_CHUNK:(c + 1) * SGU_CHUNK, lanes] for c in range(n_chunks)], axis=1)
        sv = jnp.dot(w_causal, vg, preferred_element_type=_F32) + bs_ref[0, :, g:g + 1]
        for c in range(n_chunks):
            rows = slice(c * SGU_CHUNK, (c + 1) * SGU_CHUNK)
            y_ref[rows, 2 * W + g * GROUP_WIDTH:2 * W + (g + 1) * GROUP_WIDTH] = (
                u[rows, lanes] * sv[:, c * GROUP_WIDTH:(c + 1) * GROUP_WIDTH]).astype(_BF16)

    p = jnp.dot(n, w_ref[:, 5 * W:8 * W], preferred_element_type=_F32)
    z = p[:, W:2 * W] * p[:, 2 * W:3 * W]
    sconv_buf[SCONV_HALO:SCONV_HALO + tm, :] = z
    base = SCONV_HALO - (SHORT_CONV_KERNEL - 1)
    acc = sconv_w_ref[0, SHORT_CONV_KERNEL - 1:SHORT_CONV_KERNEL, :] * z
    for k in range(SHORT_CONV_KERNEL - 1):
        acc = acc + sconv_w_ref[0, k:k + 1, :] * sconv_buf[base + k:base + k + tm, :]
    y_ref[:, 3 * W:4 * W] = (p[:, 0:W] * acc).astype(_BF16)
    _shift_history(sconv_buf, SCONV_HALO, tm)


def _mixa_call(x, ada, pre_g, mix_w_in, conv_w, conv_b, conv_ln_g, conv_ln_b, pool_group_w,
               pool_scale, sgu_ln_g, sgu_ln_b, sgu_w_s, sgu_b_s_t, sconv_w, layer):
    tm, W = MIXA_TM, BRANCH_WIDTH
    row_vec = pl.BlockSpec((1, 1, W), lambda i: (layer, 0, 0))
    return pl.pallas_call(
        _mixa_kernel,
        grid=(SEQ // tm,),
        in_specs=[
            pl.BlockSpec((tm, D_MODEL), lambda i: (i, 0)),
            pl.BlockSpec((1, 1, ADA_WIDTH), lambda i: (layer, 0, 0)),
            pl.BlockSpec((1, N_SUBLAYERS, D_MODEL), lambda i: (layer, 0, 0)),
            pl.BlockSpec((None, D_MODEL, MIX_IN_WIDTH), lambda i: (layer, 0, 0),
                         pipeline_mode=pl.Buffered(1)),
            pl.BlockSpec((1, CONV_KERNEL, W), lambda i: (layer, 0, 0)),
            row_vec, row_vec, row_vec,
            pl.BlockSpec((1, N_GROUPS, GROUP_WIDTH, GROUP_WIDTH), lambda i: (layer, 0, 0, 0)),
            row_vec, row_vec, row_vec,
            pl.BlockSpec((1, N_GROUPS, SGU_CHUNK, SGU_CHUNK), lambda i: (layer, 0, 0, 0)),
            pl.BlockSpec((1, SGU_CHUNK, N_GROUPS), lambda i: (layer, 0, 0)),
            pl.BlockSpec((1, SHORT_CONV_KERNEL, W), lambda i: (layer, 0, 0)),
        ],
        out_specs=[
            pl.BlockSpec((tm, D_MODEL), lambda i: (i, 0)),
            pl.BlockSpec((tm, N_BRANCHES * W), lambda i: (i, 0)),
        ],
        out_shape=[
            jax.ShapeDtypeStruct((SEQ, D_MODEL), _BF16),
            jax.ShapeDtypeStruct((SEQ, N_BRANCHES * W), _BF16),
        ],
        scratch_shapes=[
            pltpu.VMEM((CONV_HALO + tm, W), _F32),
            pltpu.VMEM((POOL_HALO + tm, W), _F32),
            pltpu.VMEM((SCONV_HALO + tm, W), _F32),
        ],
        compiler_params=pltpu.CompilerParams(
            dimension_semantics=("arbitrary",),
            vmem_limit_bytes=V7X_VMEM_LIMIT_BYTES),
        name=f"mix_branches_l{layer}",
    )(x, ada, pre_g, mix_w_in, conv_w, conv_b, conv_ln_g, conv_ln_b, pool_group_w, pool_scale,
      sgu_ln_g, sgu_ln_b, sgu_w_s, sgu_b_s_t, sconv_w)


def _mixb_kernel(x_ref, n_ref, y_ref, ada_ref, post_g_ref, gw_ref, gb_ref, bw_ref, wo_ref,
                 o_ref, acc_scr):
    W = BRANCH_WIDTH
    j = pl.program_id(1)

    @pl.when(j == 0)
    def _():
        acc_scr[...] = jnp.zeros_like(acc_scr)

    n = n_ref[...]
    merged = None
    for b in range(N_BRANCHES):
        gate = _sigmoid(jnp.dot(n, gw_ref[b], preferred_element_type=_F32) + gb_ref[0, b:b + 1, :])
        term = gate * jnp.dot(y_ref[:, b * W:(b + 1) * W], bw_ref[b], preferred_element_type=_F32)
        merged = term if merged is None else merged + term
    acc_scr[...] += jnp.dot(merged.astype(_BF16), wo_ref[...], preferred_element_type=_F32)

    @pl.when(j == pl.num_programs(1) - 1)
    def _():
        y = _rms_norm(acc_scr[...], post_g_ref[0, 1:2, :])
        o_ref[...] = x_ref[...] + _mod(ada_ref, 1, 2) * y


def _mixb_call(x, n, y, ada, post_g, gate_w, gate_b, branch_w_out, w_o, layer):
    tm, tn, W = MIXB_TM, MIXB_TN, BRANCH_WIDTH
    return pl.pallas_call(
        _mixb_kernel,
        grid=(SEQ // tm, D_MODEL // tn),
        in_specs=[
            pl.BlockSpec((tm, D_MODEL), lambda i, j: (i, 0)),
            pl.BlockSpec((tm, D_MODEL), lambda i, j: (i, 0)),
            pl.BlockSpec((tm, N_BRANCHES * W), lambda i, j: (i, 0)),
            pl.BlockSpec((1, 1, ADA_WIDTH), lambda i, j: (layer, 0, 0)),
            pl.BlockSpec((1, N_SUBLAYERS, D_MODEL), lambda i, j: (layer, 0, 0)),
            pl.BlockSpec((None, N_BRANCHES, D_MODEL, tn), lambda i, j: (layer, 0, 0, j)),
            pl.BlockSpec((1, N_BRANCHES, tn), lambda i, j: (layer, 0, j)),
            pl.BlockSpec((None, N_BRANCHES, W, tn), lambda i, j: (layer, 0, 0, j)),
            pl.BlockSpec((None, tn, D_MODEL), lambda i, j: (layer, j, 0)),
        ],
        out_specs=pl.BlockSpec((tm, D_MODEL), lambda i, j: (i, 0)),
        out_shape=jax.ShapeDtypeStruct((SEQ, D_MODEL), _F32),
        scratch_shapes=[pltpu.VMEM((tm, D_MODEL), _F32)],
        compiler_params=pltpu.CompilerParams(
            dimension_semantics=("arbitrary", "arbitrary"),
            vmem_limit_bytes=V7X_VMEM_LIMIT_BYTES),
        name=f"mix_merge_l{layer}",
    )(x, n, y, ada, post_g, gate_w, gate_b, branch_w_out, w_o)


def kernel(x, c, ada_w, ada_b, pre_g, post_g, ffn_w_in, ffn_w_out, mix_w_in, gate_w, gate_b, conv_w, conv_b, conv_ln_g, conv_ln_b, pool_group_w, pool_scale, sgu_ln_g, sgu_ln_b, sgu_w_s, sgu_b_s, sconv_w, branch_w_out, w_o):
    assert x.shape == (1, SEQ, D_MODEL) and c.shape == (1, D_MODEL)
    L, W = DEPTH, BRANCH_WIDTH
    h = x.reshape(SEQ, D_MODEL)
    ada = _ada_call(c.reshape(D_MODEL, 1), ada_w, ada_b.reshape(L, 1, ADA_WIDTH))

    ffn_w_in_b = ffn_w_in.astype(_BF16)
    ffn_w_out_b = ffn_w_out.astype(_BF16)
    mix_w_in_b = mix_w_in.astype(_BF16)
    gate_w_b = gate_w.astype(_BF16)
    branch_w_out_b = branch_w_out.astype(_BF16)
    w_o_b = w_o.astype(_BF16)
    pool_group_w_b = pool_group_w.astype(_BF16)
    vec = lambda a: a.reshape(L, 1, W)
    sgu_b_s_t = jnp.swapaxes(sgu_b_s, 1, 2)

    for l in range(DEPTH):
        h = _ffn_call(h, ada, pre_g, post_g, ffn_w_in_b, ffn_w_out_b, l, 0, 0)
        n, y = _mixa_call(h, ada, pre_g, mix_w_in_b, conv_w, vec(conv_b), vec(conv_ln_g),
                          vec(conv_ln_b), pool_group_w_b, vec(pool_scale), vec(sgu_ln_g),
                          vec(sgu_ln_b), sgu_w_s, sgu_b_s_t, sconv_w, l)
        h = _mixb_call(h, n, y, ada, post_g, gate_w_b, gate_b, branch_w_out_b, w_o_b, l)
        h = _ffn_call(h, ada, pre_g, post_g, ffn_w_in_b, ffn_w_out_b, l, 1, 2)
    return h.reshape(1, SEQ, D_MODEL)
```

```python
import functools

import jax
import jax.numpy as jnp
from jax import lax
from jax.experimental import pallas as pl
from jax.experimental.pallas import tpu as pltpu

D_MODEL = 2048
SEQ = 16384
DEPTH = 2
BRANCH_WIDTH = 512
N_BRANCHES = 4
GROUP_WIDTH = 128
N_GROUPS = 4
CONV_KERNEL = 31
POOL_WINDOWS = (2, 4, 8, 16)
SGU_CHUNK = 128
SHORT_CONV_KERNEL = 3
D_FF = 5632
N_SUBLAYERS = 3
N_MOD = 3
ADA_WIDTH = N_SUBLAYERS * N_MOD * D_MODEL
MIX_IN_WIDTH = 8 * BRANCH_WIDTH
RMS_EPS = 1e-6
LN_EPS = 1e-5

V7X_VMEM_LIMIT_BYTES = 56 * 1024 * 1024
SUBLANES = 8

ADA_TN = 1024
FFN_TM = 1024
FFN_TF = 512
FFN_TN = 512
ROW_CHUNK = 16
CONV_ROWS = 32
MIXA_TM = 512
MIXB_TM = 512
MIXB_TN = 512
CONV_HALO = 32
POOL_HALO = 16
SCONV_HALO = 8

_BF16 = jnp.bfloat16
_F32 = jnp.float32


def _sigmoid(v):
    return 1.0 / (1.0 + jnp.exp(-v))


def _silu(v):
    return v * _sigmoid(v)


def _rms_norm(v, g):
    return v * lax.rsqrt(jnp.mean(v * v, axis=-1, keepdims=True) + RMS_EPS) * g


def _layer_norm(v, g, b):
    mu = jnp.mean(v, axis=-1, keepdims=True)
    vc = v - mu
    return vc * lax.rsqrt(jnp.mean(vc * vc, axis=-1, keepdims=True) + LN_EPS) * g + b


def _mod(ada_ref, sub, which):
    off = (sub * N_MOD + which) * D_MODEL
    return ada_ref[0, :, off:off + D_MODEL]


def _modulated_norm(x, pre_g_ref, ada_ref, sub):
    n = _rms_norm(x, pre_g_ref[0, sub:sub + 1, :])
    return n * (1.0 + _mod(ada_ref, sub, 1)) + _mod(ada_ref, sub, 0)


def _row_chunks(tm):
    return [slice(r, r + ROW_CHUNK) for r in range(0, tm, ROW_CHUNK)]


def _residual_epilogue(x_ref, o_ref, gain, tm):
    for rows in _row_chunks(tm):
        y = o_ref[rows, :]
        inv = lax.rsqrt(jnp.mean(y * y, axis=-1, keepdims=True) + RMS_EPS)
        o_ref[rows, :] = x_ref[rows, :] + y * inv * gain


def _ada_kernel(c_ref, w_ref, b_ref, o_ref):
    cond = _silu(c_ref[...])
    o_ref[0] = jnp.sum(w_ref[0] * cond, axis=0, keepdims=True) + b_ref[0]


def _ada_call(c_col, ada_w, ada_b3):
    return pl.pallas_call(
        _ada_kernel,
        grid=(DEPTH, ADA_WIDTH // ADA_TN),
        in_specs=[
            pl.BlockSpec((D_MODEL, 1), lambda l, j: (0, 0)),
            pl.BlockSpec((1, D_MODEL, ADA_TN), lambda l, j: (l, 0, j)),
            pl.BlockSpec((1, 1, ADA_TN), lambda l, j: (l, 0, j)),
        ],
        out_specs=pl.BlockSpec((1, 1, ADA_TN), lambda l, j: (l, 0, j)),
        out_shape=jax.ShapeDtypeStruct((DEPTH, 1, ADA_WIDTH), _F32),
        compiler_params=pltpu.CompilerParams(
            dimension_semantics=("arbitrary", "arbitrary"),
            vmem_limit_bytes=V7X_VMEM_LIMIT_BYTES),
        name="ada",
    )(c_col, ada_w, ada_b3)


def _ffn_kernel(sub, x_ref, ada_ref, pre_g_ref, post_g_ref, wg_ref, wu_ref, wo_ref,
                o_ref, n_scr):
    j = pl.program_id(1)

    @pl.when(j == 0)
    def _():
        scale = pre_g_ref[0, sub:sub + 1, :] * (1.0 + _mod(ada_ref, sub, 1))
        shift = _mod(ada_ref, sub, 0)

        for rows in _row_chunks(FFN_TM):
            xv = x_ref[rows, :]
            inv = lax.rsqrt(jnp.mean(xv * xv, axis=-1, keepdims=True) + RMS_EPS)
            n_scr[rows, :] = (xv * inv * scale + shift).astype(_BF16)
            o_ref[rows, :] = jnp.zeros((ROW_CHUNK, D_MODEL), _F32)

    n = n_scr[...]
    hg = jnp.dot(n, wg_ref[...], preferred_element_type=_F32)
    hu = jnp.dot(n, wu_ref[...], preferred_element_type=_F32)
    a = (_silu(hg) * hu).astype(_BF16)
    for c in range(D_MODEL // FFN_TN):
        cols = slice(c * FFN_TN, (c + 1) * FFN_TN)
        o_ref[:, cols] += jnp.dot(a, wo_ref[:, cols], preferred_element_type=_F32)

    @pl.when(j == pl.num_programs(1) - 1)
    def _():
        _residual_epilogue(x_ref, o_ref, post_g_ref[0, sub:sub + 1, :] * (0.5 * _mod(ada_ref, sub, 2)),
                           FFN_TM)


def _ffn_call(x, ada, pre_g, post_g, w_in, w_out, layer, which, sub):
    n_f = D_FF // FFN_TF
    return pl.pallas_call(
        functools.partial(_ffn_kernel, sub),
        grid=(SEQ // FFN_TM, n_f),
        in_specs=[
            pl.BlockSpec((FFN_TM, D_MODEL), lambda i, j: (i, 0)),
            pl.BlockSpec((1, 1, ADA_WIDTH), lambda i, j: (layer, 0, 0)),
            pl.BlockSpec((1, N_SUBLAYERS, D_MODEL), lambda i, j: (layer, 0, 0)),
            pl.BlockSpec((1, N_SUBLAYERS, D_MODEL), lambda i, j: (layer, 0, 0)),
            pl.BlockSpec((None, None, D_MODEL, FFN_TF), lambda i, j: (layer, which, 0, j)),
            pl.BlockSpec((None, None, D_MODEL, FFN_TF), lambda i, j: (layer, which, 0, j + n_f)),
            pl.BlockSpec((None, None, FFN_TF, D_MODEL), lambda i, j: (layer, which, j, 0)),
        ],
        out_specs=pl.BlockSpec((FFN_TM, D_MODEL), lambda i, j: (i, 0)),
        out_shape=jax.ShapeDtypeStruct((SEQ, D_MODEL), _F32),
        scratch_shapes=[pltpu.VMEM((FFN_TM, D_MODEL), _BF16)],
        compiler_params=pltpu.CompilerParams(
            dimension_semantics=("arbitrary", "arbitrary"),
            vmem_limit_bytes=V7X_VMEM_LIMIT_BYTES),
        name=f"ffn_l{layer}_{which}",
    )(x, ada, pre_g, post_g, w_in, w_in, w_out)


def _shift_history(buf, halo, tm):
    buf[0:halo, :] = buf[tm:tm + halo, :]


def _mixa_kernel(x_ref, ada_ref, pre_g_ref, w_ref, conv_w_ref, conv_b_ref, cln_g_ref, cln_b_ref,
                 pool_w_ref, pool_s_ref, sln_g_ref, sln_b_ref, ws_ref, bs_ref, sconv_w_ref,
                 n_ref, y_ref, conv_buf, pool_buf, sconv_buf):
    tm = MIXA_TM
    W = BRANCH_WIDTH
    i = pl.program_id(0)

    @pl.when(i == 0)
    def _():
        conv_buf[0:CONV_HALO, :] = jnp.zeros((CONV_HALO, W), _F32)
        pool_buf[0:POOL_HALO, :] = jnp.zeros((POOL_HALO, W), _F32)
        sconv_buf[0:SCONV_HALO, :] = jnp.zeros((SCONV_HALO, W), _F32)

    n = _modulated_norm(x_ref[...], pre_g_ref, ada_ref, 1).astype(_BF16)
    n_ref[...] = n

    p = jnp.dot(n, w_ref[:, 0:2 * W], preferred_element_type=_F32)
    conv_buf[CONV_HALO:CONV_HALO + tm, :] = p[:, 0:W] * _sigmoid(p[:, W:2 * W])
    base = CONV_HALO - (CONV_KERNEL - 1)
    for r0 in range(0, tm, CONV_ROWS):
        acc = None
        for r in range(SUBLANES):
            n_rows = CONV_ROWS if r == 0 else CONV_ROWS + SUBLANES
            part = None
            for k in range(CONV_KERNEL):
                if (base + k) % SUBLANES != r:
                    continue
                start = r0 + (base + k) - r
                term = conv_w_ref[0, k:k + 1, :] * conv_buf[start:start + n_rows, :]
                part = term if part is None else part + term
            if r != 0:
                part = part[r:r + CONV_ROWS, :]
            acc = part if acc is None else acc + part
        acc = acc + conv_b_ref[0]
        y_ref[r0:r0 + CONV_ROWS, 0:W] = _silu(
            _layer_norm(acc, cln_g_ref[0], cln_b_ref[0])).astype(_BF16)
    _shift_history(conv_buf, CONV_HALO, tm)

    p = jnp.dot(n, w_ref[:, 2 * W:3 * W], preferred_element_type=_F32)
    pool_buf[POOL_HALO:POOL_HALO + tm, :] = p
    pos1 = (lax.broadcasted_iota(jnp.int32, (tm, 1), 0) + (i * tm + 1)).astype(_F32)
    for g, win in enumerate(POOL_WINDOWS):
        lanes = slice(g * GROUP_WIDTH, (g + 1) * GROUP_WIDTH)
        tok = p[:, lanes]
        tot = tok
        for d in range(1, win):
            tot = tot + pool_buf[POOL_HALO - d:POOL_HALO - d + tm, lanes]
        pooled = tot / jnp.minimum(pos1, float(win)) - tok
        mixed = jnp.dot(pooled.astype(_BF16), pool_w_ref[0, g], preferred_element_type=_F32)
        off = W + g * GROUP_WIDTH
        y_ref[:, off:off + GROUP_WIDTH] = (mixed * pool_s_ref[0, :, lanes]).astype(_BF16)
    _shift_history(pool_buf, POOL_HALO, tm)

    p = jnp.dot(n, w_ref[:, 3 * W:5 * W], preferred_element_type=_F32)
    p = 0.5 * p * (1.0 + lax.erf(p * (2.0 ** -0.5)))
    u = p[:, 0:W]
    v = _layer_norm(p[:, W:2 * W], sln_g_ref[0], sln_b_ref[0]).astype(_BF16)
    n_chunks = tm // SGU_CHUNK
    row = lax.broadcasted_iota(jnp.int32, (SGU_CHUNK, SGU_CHUNK), 0)
    col = lax.broadcasted_iota(jnp.int32, (SGU_CHUNK, SGU_CHUNK), 1)
    for g in range(N_GROUPS):
        lanes = slice(g * GROUP_WIDTH, (g + 1) * GROUP_WIDTH)
        w_causal = jnp.where(col <= row, ws_ref[0, g], 0.0).astype(_BF16)
        vg = jnp.concatenate(
            [v[c * SGU_CHUNK:(c + 1) * SGU_CHUNK, lanes] for c in range(n_chunks)], axis=1)
        sv = jnp.dot(w_causal, vg, preferred_element_type=_F32) + bs_ref[0, :, g:g + 1]
        for c in range(n_chunks):
            rows = slice(c * SGU_CHUNK, (c + 1) * SGU_CHUNK)
            y_ref[rows, 2 * W + g * GROUP_WIDTH:2 * W + (g + 1) * GROUP_WIDTH] = (
                u[rows, lanes] * sv[:, c * GROUP_WIDTH:(c + 1) * GROUP_WIDTH]).astype(_BF16)

    p = jnp.dot(n, w_ref[:, 5 * W:8 * W], preferred_element_type=_F32)
    z = p[:, W:2 * W] * p[:, 2 * W:3 * W]
    sconv_buf[SCONV_HALO:SCONV_HALO + tm, :] = z
    base = SCONV_HALO - (SHORT_CONV_KERNEL - 1)
    acc = sconv_w_ref[0, SHORT_CONV_KERNEL - 1:SHORT_CONV_KERNEL, :] * z
    for k in range(SHORT_CONV_KERNEL - 1):
        acc = acc + sconv_w_ref[0, k:k + 1, :] * sconv_buf[base + k:base + k + tm, :]
    y_ref[:, 3 * W:4 * W] = (p[:, 0:W] * acc).astype(_BF16)
    _shift_history(sconv_buf, SCONV_HALO, tm)


def _mixa_call(x, ada, pre_g, mix_w_in, conv_w, conv_b, conv_ln_g, conv_ln_b, pool_group_w,
               pool_scale, sgu_ln_g, sgu_ln_b, sgu_w_s, sgu_b_s_t, sconv_w, layer):
    tm, W = MIXA_TM, BRANCH_WIDTH
    row_vec = pl.BlockSpec((1, 1, W), lambda i: (layer, 0, 0))
    return pl.pallas_call(
        _mixa_kernel,
        grid=(SEQ // tm,),
        in_specs=[
            pl.BlockSpec((tm, D_MODEL), lambda i: (i, 0)),
            pl.BlockSpec((1, 1, ADA_WIDTH), lambda i: (layer, 0, 0)),
            pl.BlockSpec((1, N_SUBLAYERS, D_MODEL), lambda i: (layer, 0, 0)),
            pl.BlockSpec((None, D_MODEL, MIX_IN_WIDTH), lambda i: (layer, 0, 0),
                         pipeline_mode=pl.Buffered(1)),
            pl.BlockSpec((1, CONV_KERNEL, W), lambda i: (layer, 0, 0)),
            row_vec, row_vec, row_vec,
            pl.BlockSpec((1, N_GROUPS, GROUP_WIDTH, GROUP_WIDTH), lambda i: (layer, 0, 0, 0)),
            row_vec, row_vec, row_vec,
            pl.BlockSpec((1, N_GROUPS, SGU_CHUNK, SGU_CHUNK), lambda i: (layer, 0, 0, 0)),
            pl.BlockSpec((1, SGU_CHUNK, N_GROUPS), lambda i: (layer, 0, 0)),
            pl.BlockSpec((1, SHORT_CONV_KERNEL, W), lambda i: (layer, 0, 0)),
        ],
        out_specs=[
            pl.BlockSpec((tm, D_MODEL), lambda i: (i, 0)),
            pl.BlockSpec((tm, N_BRANCHES * W), lambda i: (i, 0)),
        ],
        out_shape=[
            jax.ShapeDtypeStruct((SEQ, D_MODEL), _BF16),
            jax.ShapeDtypeStruct((SEQ, N_BRANCHES * W), _BF16),
        ],
        scratch_shapes=[
            pltpu.VMEM((CONV_HALO + tm, W), _F32),
            pltpu.VMEM((POOL_HALO + tm, W), _F32),
            pltpu.VMEM((SCONV_HALO + tm, W), _F32),
        ],
        compiler_params=pltpu.CompilerParams(
            dimension_semantics=("arbitrary",),
            vmem_limit_bytes=V7X_VMEM_LIMIT_BYTES),
        name=f"mix_branches_l{layer}",
    )(x, ada, pre_g, mix_w_in, conv_w, conv_b, conv_ln_g, conv_ln_b, pool_group_w, pool_scale,
      sgu_ln_g, sgu_ln_b, sgu_w_s, sgu_b_s_t, sconv_w)


def _mixb_kernel(x_ref, n_ref, y_ref, ada_ref, post_g_ref, gw_ref, gb_ref, bw_ref, wo_ref,
                 o_ref):
    W = BRANCH_WIDTH
    j = pl.program_id(1)

    @pl.when(j == 0)
    def _():
        o_ref[...] = jnp.zeros_like(o_ref)

    n = n_ref[...]
    merged = None
    for b in range(N_BRANCHES):
        gate = _sigmoid(jnp.dot(n, gw_ref[b], preferred_element_type=_F32) + gb_ref[0, b:b + 1, :])
        term = gate * jnp.dot(y_ref[:, b * W:(b + 1) * W], bw_ref[b], preferred_element_type=_F32)
        merged = term if merged is None else merged + term
    merged = merged.astype(_BF16)
    for c in range(D_MODEL // FFN_TN):
        cols = slice(c * FFN_TN, (c + 1) * FFN_TN)
        o_ref[:, cols] += jnp.dot(merged, wo_ref[:, cols], preferred_element_type=_F32)

    @pl.when(j == pl.num_programs(1) - 1)
    def _():
        _residual_epilogue(x_ref, o_ref, post_g_ref[0, 1:2, :] * _mod(ada_ref, 1, 2), MIXB_TM)


def _mixb_call(x, n, y, ada, post_g, gate_w, gate_b, branch_w_out, w_o, layer):
    tm, tn, W = MIXB_TM, MIXB_TN, BRANCH_WIDTH
    return pl.pallas_call(
        _mixb_kernel,
        grid=(SEQ // tm, D_MODEL // tn),
        in_specs=[
            pl.BlockSpec((tm, D_MODEL), lambda i, j: (i, 0)),
            pl.BlockSpec((tm, D_MODEL), lambda i, j: (i, 0)),
            pl.BlockSpec((tm, N_BRANCHES * W), lambda i, j: (i, 0)),
            pl.BlockSpec((1, 1, ADA_WIDTH), lambda i, j: (layer, 0, 0)),
            pl.BlockSpec((1, N_SUBLAYERS, D_MODEL), lambda i, j: (layer, 0, 0)),
            pl.BlockSpec((None, N_BRANCHES, D_MODEL, tn), lambda i, j: (layer, 0, 0, j)),
            pl.BlockSpec((1, N_BRANCHES, tn), lambda i, j: (layer, 0, j)),
            pl.BlockSpec((None, N_BRANCHES, W, tn), lambda i, j: (layer, 0, 0, j)),
            pl.BlockSpec((None, tn, D_MODEL), lambda i, j: (layer, j, 0)),
        ],
        out_specs=pl.BlockSpec((tm, D_MODEL), lambda i, j: (i, 0)),
        out_shape=jax.ShapeDtypeStruct((SEQ, D_MODEL), _F32),
        compiler_params=pltpu.CompilerParams(
            dimension_semantics=("arbitrary", "arbitrary"),
            vmem_limit_bytes=V7X_VMEM_LIMIT_BYTES),
        name=f"mix_merge_l{layer}",
    )(x, n, y, ada, post_g, gate_w, gate_b, branch_w_out, w_o)


def kernel(x, c, ada_w, ada_b, pre_g, post_g, ffn_w_in, ffn_w_out, mix_w_in, gate_w, gate_b, conv_w, conv_b, conv_ln_g, conv_ln_b, pool_group_w, pool_scale, sgu_ln_g, sgu_ln_b, sgu_w_s, sgu_b_s, sconv_w, branch_w_out, w_o):
    assert x.shape == (1, SEQ, D_MODEL) and c.shape == (1, D_MODEL)
    L, W = DEPTH, BRANCH_WIDTH
    h = x.reshape(SEQ, D_MODEL)
    ada = _ada_call(c.reshape(D_MODEL, 1), ada_w, ada_b.reshape(L, 1, ADA_WIDTH))

    ffn_w_in_b = ffn_w_in.astype(_BF16)
    ffn_w_out_b = ffn_w_out.astype(_BF16)
    mix_w_in_b = mix_w_in.astype(_BF16)
    gate_w_b = gate_w.astype(_BF16)
    branch_w_out_b = branch_w_out.astype(_BF16)
    w_o_b = w_o.astype(_BF16)
    pool_group_w_b = pool_group_w.astype(_BF16)
    vec = lambda a: a.reshape(L, 1, W)
    sgu_b_s_t = jnp.swapaxes(sgu_b_s, 1, 2)

    for l in range(DEPTH):
        h = _ffn_call(h, ada, pre_g, post_g, ffn_w_in_b, ffn_w_out_b, l, 0, 0)
        n, y = _mixa_call(h, ada, pre_g, mix_w_in_b, conv_w, vec(conv_b), vec(conv_ln_g),
                          vec(conv_ln_b), pool_group_w_b, vec(pool_scale), vec(sgu_ln_g),
                          vec(sgu_ln_b), sgu_w_s, sgu_b_s_t, sconv_w, l)
        h = _mixb_call(h, n, y, ada, post_g, gate_w_b, gate_b, branch_w_out_b, w_o_b, l)
        h = _ffn_call(h, ada, pre_g, post_g, ffn_w_in_b, ffn_w_out_b, l, 1, 2)
    return h.reshape(1, SEQ, D_MODEL)
```

```python
import functools
from typing import NamedTuple

import jax
import jax.numpy as jnp
from jax import lax
from jax.experimental import pallas as pl
from jax.experimental.pallas import tpu as pltpu

D_MODEL = 2048
SEQ = 16384
DEPTH = 2
BRANCH_WIDTH = 512
N_BRANCHES = 4
GROUP_WIDTH = 128
N_GROUPS = 4
CONV_KERNEL = 31
POOL_WINDOWS = (2, 4, 8, 16)
SGU_CHUNK = 128
SHORT_CONV_KERNEL = 3
D_FF = 5632
N_SUBLAYERS = 3
N_MOD = 3
ADA_WIDTH = N_SUBLAYERS * N_MOD * D_MODEL
MIX_IN_WIDTH = 8 * BRANCH_WIDTH
RMS_EPS = 1e-6
LN_EPS = 1e-5

V7X_VMEM_LIMIT_BYTES = 60 * 1024 * 1024
SUBLANES = 8
BF16_SUBLANES = 16

ADA_TN = 1024
FFN_TM = 1024
FFN_TF = 512
FFN_TN = 512
ROW_CHUNK = 16
CONV_ROWS = 32
MIXA_TM = 512
MIXB_TM = 512
MIXB_TN = 512
CONV_HALO = 32
POOL_HALO = 16
SCONV_HALO = 8

_BF16 = jnp.bfloat16
_F32 = jnp.float32


def _sigmoid(v):
    return 1.0 / (1.0 + jnp.exp(-v))


def _silu(v):
    return v * _sigmoid(v)


def _rms_norm(v, g):
    return v * lax.rsqrt(jnp.mean(v * v, axis=-1, keepdims=True) + RMS_EPS) * g


def _layer_norm(v, g, b):
    mu = jnp.mean(v, axis=-1, keepdims=True)
    vc = v - mu
    return vc * lax.rsqrt(jnp.mean(vc * vc, axis=-1, keepdims=True) + LN_EPS) * g + b


def _mod(ada_ref, sub, which):
    off = (sub * N_MOD + which) * D_MODEL
    return ada_ref[0, :, off:off + D_MODEL]


def _modulated_norm(x, pre_g_ref, ada_ref, sub):
    n = _rms_norm(x, pre_g_ref[0, sub:sub + 1, :])
    return n * (1.0 + _mod(ada_ref, sub, 1)) + _mod(ada_ref, sub, 0)


def _row_chunks(tm):
    return [slice(r, r + ROW_CHUNK) for r in range(0, tm, ROW_CHUNK)]


def _residual_epilogue(x_ref, o_ref, gain, tm):
    for rows in _row_chunks(tm):
        y = o_ref[rows, :]
        inv = lax.rsqrt(jnp.mean(y * y, axis=-1, keepdims=True) + RMS_EPS)
        o_ref[rows, :] = x_ref[rows, :] + y * inv * gain


class _CastJob(NamedTuple):
    src: jax.Array
    prefix: tuple
    block_rows: int

    @property
    def rows(self):
        return self.src.shape[-2]

    @property
    def cols(self):
        return self.src.shape[-1]

    @property
    def n_blocks(self):
        return self.rows // self.block_rows


def _cast_job(src, prefix, n_steps):
    rows = src.shape[-2]
    block_rows = next(b for b in range(BF16_SUBLANES, rows + 1, BF16_SUBLANES)
                      if rows % b == 0 and rows // b <= n_steps)
    return _CastJob(src, tuple(prefix), block_rows)


def _cast_specs(jobs, grid):
    n_steps = 1
    for g in grid:
        n_steps *= g
    in_specs, out_specs, out_shapes = [], [], []
    for job in jobs:
        assert job.rows % job.block_rows == 0 and job.n_blocks <= n_steps
        last = job.n_blocks - 1

        def block_index(*ids, last=last):
            step = ids[0]
            for g, idx in zip(grid[1:], ids[1:]):
                step = step * g + idx
            return jnp.minimum(step, last)

        in_specs.append(pl.BlockSpec(
            (None,) * len(job.prefix) + (job.block_rows, job.cols),
            lambda *ids, job=job, block_index=block_index: job.prefix + (block_index(*ids), 0)))
        out_specs.append(pl.BlockSpec(
            (job.block_rows, job.cols), lambda *ids, block_index=block_index: (block_index(*ids), 0)))
        out_shapes.append(jax.ShapeDtypeStruct((job.rows, job.cols), _BF16))
    return in_specs, out_specs, out_shapes


def _run_casts(src_refs, dst_refs):
    for src, dst in zip(src_refs, dst_refs):
        dst[...] = src[...].astype(_BF16)


def _ada_kernel(c_ref, w_ref, b_ref, o_ref):
    cond = _silu(c_ref[...])
    o_ref[0] = jnp.sum(w_ref[0] * cond, axis=0, keepdims=True) + b_ref[0]


def _ada_call(c_col, ada_w, ada_b3):
    return pl.pallas_call(
        _ada_kernel,
        grid=(DEPTH, ADA_WIDTH // ADA_TN),
        in_specs=[
            pl.BlockSpec((D_MODEL, 1), lambda l, j: (0, 0)),
            pl.BlockSpec((1, D_MODEL, ADA_TN), lambda l, j: (l, 0, j)),
            pl.BlockSpec((1, 1, ADA_TN), lambda l, j: (l, 0, j)),
        ],
        out_specs=pl.BlockSpec((1, 1, ADA_TN), lambda l, j: (l, 0, j)),
        out_shape=jax.ShapeDtypeStruct((DEPTH, 1, ADA_WIDTH), _F32),
        compiler_params=pltpu.CompilerParams(
            dimension_semantics=("arbitrary", "arbitrary"),
            vmem_limit_bytes=V7X_VMEM_LIMIT_BYTES),
        name="ada",
    )(c_col, ada_w, ada_b3)


def _ffn_kernel(sub, n_cast, x_ref, ada_ref, pre_g_ref, post_g_ref, wg_ref, wu_ref, wo_ref, *rest):
    cast_src, (o_ref, *cast_dst), (n_scr,) = (
        rest[:n_cast], rest[n_cast:2 * n_cast + 1], rest[2 * n_cast + 1:])
    j = pl.program_id(1)

    @pl.when(j == 0)
    def _():
        scale = pre_g_ref[0, sub:sub + 1, :] * (1.0 + _mod(ada_ref, sub, 1))
        shift = _mod(ada_ref, sub, 0)

        for rows in _row_chunks(FFN_TM):
            xv = x_ref[rows, :]
            inv = lax.rsqrt(jnp.mean(xv * xv, axis=-1, keepdims=True) + RMS_EPS)
            n_scr[rows, :] = (xv * inv * scale + shift).astype(_BF16)
            o_ref[rows, :] = jnp.zeros((ROW_CHUNK, D_MODEL), _F32)

    n = n_scr[...]
    hg = jnp.dot(n, wg_ref[...], preferred_element_type=_F32)
    hu = jnp.dot(n, wu_ref[...], preferred_element_type=_F32)
    a = (_silu(hg) * hu).astype(_BF16)
    for c in range(D_MODEL // FFN_TN):
        cols = slice(c * FFN_TN, (c + 1) * FFN_TN)
        o_ref[:, cols] += jnp.dot(a, wo_ref[:, cols], preferred_element_type=_F32)
    _run_casts(cast_src, cast_dst)

    @pl.when(j == pl.num_programs(1) - 1)
    def _():
        _residual_epilogue(x_ref, o_ref, post_g_ref[0, sub:sub + 1, :] * (0.5 * _mod(ada_ref, sub, 2)),
                           FFN_TM)


def _ffn_call(x, ada, pre_g, post_g, w_in, w_out, layer, which, sub, cast_jobs=()):
    n_f = D_FF // FFN_TF
    grid = (SEQ // FFN_TM, n_f)
    cast_in, cast_out, cast_shapes = _cast_specs(cast_jobs, grid)
    return pl.pallas_call(
        functools.partial(_ffn_kernel, sub, len(cast_jobs)),
        grid=grid,
        in_specs=[
            pl.BlockSpec((FFN_TM, D_MODEL), lambda i, j: (i, 0)),
            pl.BlockSpec((1, 1, ADA_WIDTH), lambda i, j: (layer, 0, 0)),
            pl.BlockSpec((1, N_SUBLAYERS, D_MODEL), lambda i, j: (layer, 0, 0)),
            pl.BlockSpec((1, N_SUBLAYERS, D_MODEL), lambda i, j: (layer, 0, 0)),
            pl.BlockSpec((D_MODEL, FFN_TF), lambda i, j: (0, j)),
            pl.BlockSpec((D_MODEL, FFN_TF), lambda i, j: (0, j + n_f)),
            pl.BlockSpec((FFN_TF, D_MODEL), lambda i, j: (j, 0)),
        ] + cast_in,
        out_specs=[pl.BlockSpec((FFN_TM, D_MODEL), lambda i, j: (i, 0))] + cast_out,
        out_shape=[jax.ShapeDtypeStruct((SEQ, D_MODEL), _F32)] + cast_shapes,
        scratch_shapes=[pltpu.VMEM((FFN_TM, D_MODEL), _BF16)],
        compiler_params=pltpu.CompilerParams(
            dimension_semantics=("arbitrary", "arbitrary"),
            vmem_limit_bytes=V7X_VMEM_LIMIT_BYTES),
        name=f"ffn_l{layer}_{which}",
    )(x, ada, pre_g, post_g, w_in, w_in, w_out, *[job.src for job in cast_jobs])


def _shift_history(buf, halo, tm):
    buf[0:halo, :] = buf[tm:tm + halo, :]


def _mixa_kernel(n_cast, x_ref, ada_ref, pre_g_ref, w_ref, conv_w_ref, conv_b_ref, cln_g_ref,
                 cln_b_ref, pool_w_ref, pool_s_ref, sln_g_ref, sln_b_ref, ws_ref, bs_ref,
                 sconv_w_ref, *rest):
    cast_src, (n_ref, y_ref, *cast_dst), (conv_buf, pool_buf, sconv_buf) = (
        rest[:n_cast], rest[n_cast:2 * n_cast + 2], rest[2 * n_cast + 2:])
    tm = MIXA_TM
    W = BRANCH_WIDTH
    i = pl.program_id(0)

    @pl.when(i == 0)
    def _():
        conv_buf[0:CONV_HALO, :] = jnp.zeros((CONV_HALO, W), _F32)
        pool_buf[0:POOL_HALO, :] = jnp.zeros((POOL_HALO, W), _F32)
        sconv_buf[0:SCONV_HALO, :] = jnp.zeros((SCONV_HALO, W), _F32)

    n = _modulated_norm(x_ref[...], pre_g_ref, ada_ref, 1).astype(_BF16)
    n_ref[...] = n

    p = jnp.dot(n, w_ref[:, 0:2 * W], preferred_element_type=_F32)
    conv_buf[CONV_HALO:CONV_HALO + tm, :] = p[:, 0:W] * _sigmoid(p[:, W:2 * W])
    base = CONV_HALO - (CONV_KERNEL - 1)
    for r0 in range(0, tm, CONV_ROWS):
        acc = None
        for r in range(SUBLANES):
            n_rows = CONV_ROWS if r == 0 else CONV_ROWS + SUBLANES
            part = None
            for k in range(CONV_KERNEL):
                if (base + k) % SUBLANES != r:
                    continue
                start = r0 + (base + k) - r
                term = conv_w_ref[0, k:k + 1, :] * conv_buf[start:start + n_rows, :]
                part = term if part is None else part + term
            if r != 0:
                part = part[r:r + CONV_ROWS, :]
            acc = part if acc is None else acc + part
        acc = acc + conv_b_ref[0]
        y_ref[r0:r0 + CONV_ROWS, 0:W] = _silu(
            _layer_norm(acc, cln_g_ref[0], cln_b_ref[0])).astype(_BF16)
    _shift_history(conv_buf, CONV_HALO, tm)

    p = jnp.dot(n, w_ref[:, 2 * W:3 * W], preferred_element_type=_F32)
    pool_buf[POOL_HALO:POOL_HALO + tm, :] = p
    pos1 = (lax.broadcasted_iota(jnp.int32, (tm, 1), 0) + (i * tm + 1)).astype(_F32)
    for g, win in enumerate(POOL_WINDOWS):
        lanes = slice(g * GROUP_WIDTH, (g + 1) * GROUP_WIDTH)
        tok = p[:, lanes]
        tot = tok
        for d in range(1, win):
            tot = tot + pool_buf[POOL_HALO - d:POOL_HALO - d + tm, lanes]
        pooled = tot / jnp.minimum(pos1, float(win)) - tok
        mixed = jnp.dot(pooled.astype(_BF16), pool_w_ref[0, g].astype(_BF16),
                        preferred_element_type=_F32)
        off = W + g * GROUP_WIDTH
        y_ref[:, off:off + GROUP_WIDTH] = (mixed * pool_s_ref[0, :, lanes]).astype(_BF16)
    _shift_history(pool_buf, POOL_HALO, tm)

    p = jnp.dot(n, w_ref[:, 3 * W:5 * W], preferred_element_type=_F32)
    p = 0.5 * p * (1.0 + lax.erf(p * (2.0 ** -0.5)))
    u = p[:, 0:W]
    v = _layer_norm(p[:, W:2 * W], sln_g_ref[0], sln_b_ref[0]).astype(_BF16)
    n_chunks = tm // SGU_CHUNK
    row = lax.broadcasted_iota(jnp.int32, (SGU_CHUNK, SGU_CHUNK), 0)
    col = lax.broadcasted_iota(jnp.int32, (SGU_CHUNK, SGU_CHUNK), 1)
    for g in range(N_GROUPS):
        lanes = slice(g * GROUP_WIDTH, (g + 1) * GROUP_WIDTH)
        w_causal = jnp.where(col <= row, ws_ref[0, g], 0.0).astype(_BF16)
        vg = jnp.concatenate(
            [v[c * SGU_CHUNK:(c + 1) * SGU_CHUNK, lanes] for c in range(n_chunks)], axis=1)
        sv = jnp.dot(w_causal, vg, preferred_element_type=_F32) + bs_ref[0, :, g:g + 1]
        for c in range(n_chunks):
            rows = slice(c * SGU_CHUNK, (c + 1) * SGU_CHUNK)
            y_ref[rows, 2 * W + g * GROUP_WIDTH:2 * W + (g + 1) * GROUP_WIDTH] = (
                u[rows, lanes] * sv[:, c * GROUP_WIDTH:(c + 1) * GROUP_WIDTH]).astype(_BF16)

    p = jnp.dot(n, w_ref[:, 5 * W:8 * W], preferred_element_type=_F32)
    z = p[:, W:2 * W] * p[:, 2 * W:3 * W]
    sconv_buf[SCONV_HALO:SCONV_HALO + tm, :] = z
    base = SCONV_HALO - (SHORT_CONV_KERNEL - 1)
    acc = sconv_w_ref[0, SHORT_CONV_KERNEL - 1:SHORT_CONV_KERNEL, :] * z
    for k in range(SHORT_CONV_KERNEL - 1):
        acc = acc + sconv_w_ref[0, k:k + 1, :] * sconv_buf[base + k:base + k + tm, :]
    y_ref[:, 3 * W:4 * W] = (p[:, 0:W] * acc).astype(_BF16)
    _shift_history(sconv_buf, SCONV_HALO, tm)
    _run_casts(cast_src, cast_dst)


def _mixa_call(x, ada, pre_g, mix_w_in, conv_w, conv_b, conv_ln_g, conv_ln_b, pool_group_w,
               pool_scale, sgu_ln_g, sgu_ln_b, sgu_w_s, sgu_b_s_t, sconv_w, layer, cast_jobs=()):
    tm, W = MIXA_TM, BRANCH_WIDTH
    grid = (SEQ // tm,)
    cast_in, cast_out, cast_shapes = _cast_specs(cast_jobs, grid)
    row_vec = pl.BlockSpec((1, 1, W), lambda i: (layer, 0, 0))
    return pl.pallas_call(
        functools.partial(_mixa_kernel, len(cast_jobs)),
        grid=grid,
        in_specs=[
            pl.BlockSpec((tm, D_MODEL), lambda i: (i, 0)),
            pl.BlockSpec((1, 1, ADA_WIDTH), lambda i: (layer, 0, 0)),
            pl.BlockSpec((1, N_SUBLAYERS, D_MODEL), lambda i: (layer, 0, 0)),
            pl.BlockSpec((D_MODEL, MIX_IN_WIDTH), lambda i: (0, 0), pipeline_mode=pl.Buffered(1)),
            pl.BlockSpec((1, CONV_KERNEL, W), lambda i: (layer, 0, 0)),
            row_vec, row_vec, row_vec,
            pl.BlockSpec((1, N_GROUPS, GROUP_WIDTH, GROUP_WIDTH), lambda i: (layer, 0, 0, 0)),
            row_vec, row_vec, row_vec,
            pl.BlockSpec((1, N_GROUPS, SGU_CHUNK, SGU_CHUNK), lambda i: (layer, 0, 0, 0)),
            pl.BlockSpec((1, SGU_CHUNK, N_GROUPS), lambda i: (layer, 0, 0)),
            pl.BlockSpec((1, SHORT_CONV_KERNEL, W), lambda i: (layer, 0, 0)),
        ] + cast_in,
        out_specs=[
            pl.BlockSpec((tm, D_MODEL), lambda i: (i, 0)),
            pl.BlockSpec((tm, N_BRANCHES * W), lambda i: (i, 0)),
        ] + cast_out,
        out_shape=[
            jax.ShapeDtypeStruct((SEQ, D_MODEL), _BF16),
            jax.ShapeDtypeStruct((SEQ, N_BRANCHES * W), _BF16),
        ] + cast_shapes,
        scratch_shapes=[
            pltpu.VMEM((CONV_HALO + tm, W), _F32),
            pltpu.VMEM((POOL_HALO + tm, W), _F32),
            pltpu.VMEM((SCONV_HALO + tm, W), _F32),
        ],
        compiler_params=pltpu.CompilerParams(
            dimension_semantics=("arbitrary",),
            vmem_limit_bytes=V7X_VMEM_LIMIT_BYTES),
        name=f"mix_branches_l{layer}",
    )(x, ada, pre_g, mix_w_in, conv_w, conv_b, conv_ln_g, conv_ln_b, pool_group_w, pool_scale,
      sgu_ln_g, sgu_ln_b, sgu_w_s, sgu_b_s_t, sconv_w, *[job.src for job in cast_jobs])


def _mixb_kernel(n_cast, x_ref, n_ref, y_ref, ada_ref, post_g_ref, gw_ref, gb_ref, bw_ref, wo_ref,
                 *rest):
    cast_src, (o_ref, *cast_dst) = rest[:n_cast], rest[n_cast:]
    W = BRANCH_WIDTH
    j = pl.program_id(1)

    @pl.when(j == 0)
    def _():
        o_ref[...] = jnp.zeros_like(o_ref)

    n = n_ref[...]
    merged = None
    for b in range(N_BRANCHES):
        gate = _sigmoid(jnp.dot(n, gw_ref[b], preferred_element_type=_F32) + gb_ref[0, b:b + 1, :])
        term = gate * jnp.dot(y_ref[:, b * W:(b + 1) * W], bw_ref[b], preferred_element_type=_F32)
        merged = term if merged is None else merged + term
    merged = merged.astype(_BF16)
    for c in range(D_MODEL // FFN_TN):
        cols = slice(c * FFN_TN, (c + 1) * FFN_TN)
        o_ref[:, cols] += jnp.dot(merged, wo_ref[:, cols], preferred_element_type=_F32)
    _run_casts(cast_src, cast_dst)

    @pl.when(j == pl.num_programs(1) - 1)
    def _():
        _residual_epilogue(x_ref, o_ref, post_g_ref[0, 1:2, :] * _mod(ada_ref, 1, 2), MIXB_TM)


def _mixb_call(x, n, y, ada, post_g, gate_w, gate_b, branch_w_out, w_o, layer, cast_jobs=()):
    tm, tn, W = MIXB_TM, MIXB_TN, BRANCH_WIDTH
    grid = (SEQ // tm, D_MODEL // tn)
    cast_in, cast_out, cast_shapes = _cast_specs(cast_jobs, grid)
    return pl.pallas_call(
        functools.partial(_mixb_kernel, len(cast_jobs)),
        grid=grid,
        in_specs=[
            pl.BlockSpec((tm, D_MODEL), lambda i, j: (i, 0)),
            pl.BlockSpec((tm, D_MODEL), lambda i, j: (i, 0)),
            pl.BlockSpec((tm, N_BRANCHES * W), lambda i, j: (i, 0)),
            pl.BlockSpec((1, 1, ADA_WIDTH), lambda i, j: (layer, 0, 0)),
            pl.BlockSpec((1, N_SUBLAYERS, D_MODEL), lambda i, j: (layer, 0, 0)),
            pl.BlockSpec((N_BRANCHES, D_MODEL, tn), lambda i, j: (0, 0, j)),
            pl.BlockSpec((1, N_BRANCHES, tn), lambda i, j: (layer, 0, j)),
            pl.BlockSpec((N_BRANCHES, W, tn), lambda i, j: (0, 0, j)),
            pl.BlockSpec((tn, D_MODEL), lambda i, j: (j, 0)),
        ] + cast_in,
        out_specs=[pl.BlockSpec((tm, D_MODEL), lambda i, j: (i, 0))] + cast_out,
        out_shape=[jax.ShapeDtypeStruct((SEQ, D_MODEL), _F32)] + cast_shapes,
        compiler_params=pltpu.CompilerParams(
            dimension_semantics=("arbitrary", "arbitrary"),
            vmem_limit_bytes=V7X_VMEM_LIMIT_BYTES),
        name=f"mix_merge_l{layer}",
    )(x, n, y, ada, post_g, gate_w, gate_b, branch_w_out, w_o, *[job.src for job in cast_jobs])


def kernel(x, c, ada_w, ada_b, pre_g, post_g, ffn_w_in, ffn_w_out, mix_w_in, gate_w, gate_b, conv_w, conv_b, conv_ln_g, conv_ln_b, pool_group_w, pool_scale, sgu_ln_g, sgu_ln_b, sgu_w_s, sgu_b_s, sconv_w, branch_w_out, w_o):
    assert x.shape == (1, SEQ, D_MODEL) and c.shape == (1, D_MODEL)
    L, W = DEPTH, BRANCH_WIDTH
    h = x.reshape(SEQ, D_MODEL)
    ada = _ada_call(c.reshape(D_MODEL, 1), ada_w, ada_b.reshape(L, 1, ADA_WIDTH))

    vec = lambda a: a.reshape(L, 1, W)
    sgu_b_s_t = jnp.swapaxes(sgu_b_s, 1, 2)
    gate_w_rows = gate_w.reshape(L, N_BRANCHES * D_MODEL, D_MODEL)
    branch_w_rows = branch_w_out.reshape(L, N_BRANCHES * W, D_MODEL)
    ffn_steps = (SEQ // FFN_TM) * (D_FF // FFN_TF)
    mixa_steps = SEQ // MIXA_TM
    mixb_steps = (SEQ // MIXB_TM) * (D_MODEL // MIXB_TN)

    w_in_b = ffn_w_in[0, 0].astype(_BF16)
    w_out_b = ffn_w_out[0, 0].astype(_BF16)
    for l in range(DEPTH):
        h, mix_in_b, gate_b16, branch_b16, w_o_b = _ffn_call(
            h, ada, pre_g, post_g, w_in_b, w_out_b, l, 0, 0, cast_jobs=(
                _cast_job(mix_w_in, (l,), ffn_steps), _cast_job(gate_w_rows, (l,), ffn_steps),
                _cast_job(branch_w_rows, (l,), ffn_steps), _cast_job(w_o, (l,), ffn_steps)))
        n, y, w_out_b = _mixa_call(
            h, ada, pre_g, mix_in_b, conv_w, vec(conv_b), vec(conv_ln_g), vec(conv_ln_b),
            pool_group_w, vec(pool_scale), vec(sgu_ln_g), vec(sgu_ln_b), sgu_w_s, sgu_b_s_t,
            sconv_w, l, cast_jobs=(_cast_job(ffn_w_out, (l, 1), mixa_steps),))
        h, w_in_b = _mixb_call(
            h, n, y, ada, post_g, gate_b16.reshape(N_BRANCHES, D_MODEL, D_MODEL), gate_b,
            branch_b16.reshape(N_BRANCHES, W, D_MODEL), w_o_b, l,
            cast_jobs=(_cast_job(ffn_w_in, (l, 1), mixb_steps),))
        next_jobs = () if l + 1 == DEPTH else (
            _cast_job(ffn_w_in, (l + 1, 0), ffn_steps), _cast_job(ffn_w_out, (l + 1, 0), ffn_steps))
        h, *next_w = _ffn_call(h, ada, pre_g, post_g, w_in_b, w_out_b, l, 1, 2, cast_jobs=next_jobs)
        if next_w:
            w_in_b, w_out_b = next_w
    return h.reshape(1, SEQ, D_MODEL)
```

```python
import functools
from typing import NamedTuple

import jax
import jax.numpy as jnp
from jax import lax
from jax.experimental import pallas as pl
from jax.experimental.pallas import tpu as pltpu

D_MODEL = 2048
SEQ = 16384
DEPTH = 2
BRANCH_WIDTH = 512
N_BRANCHES = 4
GROUP_WIDTH = 128
N_GROUPS = 4
CONV_KERNEL = 31
POOL_WINDOWS = (2, 4, 8, 16)
SGU_CHUNK = 128
SHORT_CONV_KERNEL = 3
D_FF = 5632
N_SUBLAYERS = 3
N_MOD = 3
ADA_WIDTH = N_SUBLAYERS * N_MOD * D_MODEL
MIX_IN_WIDTH = 8 * BRANCH_WIDTH
RMS_EPS = 1e-6
LN_EPS = 1e-5

V7X_VMEM_LIMIT_BYTES = 60 * 1024 * 1024
SUBLANES = 8
BF16_SUBLANES = 16

ADA_TN = 1024
FFN_TM = 1024
FFN_TF = 512
FFN_TN = 512
ROW_CHUNK = 16
ROW_BLOCK = 256
CONV_ROWS = 32
MIXA_TM = 512
MIXB_TM = 512
MIXB_TN = 512
CONV_HALO = 32
POOL_HALO = 16
SCONV_HALO = 8

_BF16 = jnp.bfloat16
_F32 = jnp.float32


def _sigmoid(v):
    return 1.0 / (1.0 + jnp.exp(-v))


def _silu(v):
    return v * _sigmoid(v)


def _rms_norm(v, g):
    return v * lax.rsqrt(jnp.mean(v * v, axis=-1, keepdims=True) + RMS_EPS) * g


def _layer_norm(v, g, b):
    mu = jnp.mean(v, axis=-1, keepdims=True)
    vc = v - mu
    return vc * lax.rsqrt(jnp.mean(vc * vc, axis=-1, keepdims=True) + LN_EPS) * g + b


def _mod(ada_ref, sub, which):
    off = (sub * N_MOD + which) * D_MODEL
    return ada_ref[0, :, off:off + D_MODEL]


def _modulated_norm(x, pre_g_ref, ada_ref, sub):
    n = _rms_norm(x, pre_g_ref[0, sub:sub + 1, :])
    return n * (1.0 + _mod(ada_ref, sub, 1)) + _mod(ada_ref, sub, 0)


def _row_blocks(tm):
    return [slice(r, r + ROW_BLOCK) for r in range(0, tm, ROW_BLOCK)]


def _row_chunks(block):
    return [slice(r, r + ROW_CHUNK) for r in range(block.start, block.stop, ROW_CHUNK)]


def _residual_epilogue(x_ref, o_ref, gain, block):
    for rows in _row_chunks(block):
        y = o_ref[rows, :]
        inv = lax.rsqrt(jnp.mean(y * y, axis=-1, keepdims=True) + RMS_EPS)
        o_ref[rows, :] = x_ref[rows, :] + y * inv * gain


class _CastJob(NamedTuple):
    src: jax.Array
    prefix: tuple
    block_rows: int

    @property
    def rows(self):
        return self.src.shape[-2]

    @property
    def cols(self):
        return self.src.shape[-1]

    @property
    def n_blocks(self):
        return self.rows // self.block_rows


def _cast_job(src, prefix, n_steps):
    rows = src.shape[-2]
    block_rows = next(b for b in range(BF16_SUBLANES, rows + 1, BF16_SUBLANES)
                      if rows % b == 0 and rows // b <= n_steps)
    return _CastJob(src, tuple(prefix), block_rows)


def _cast_specs(jobs, grid):
    n_steps = 1
    for g in grid:
        n_steps *= g
    in_specs, out_specs, out_shapes = [], [], []
    for job in jobs:
        assert job.rows % job.block_rows == 0 and job.n_blocks <= n_steps
        last = job.n_blocks - 1

        def block_index(*ids, last=last):
            step = ids[0]
            for g, idx in zip(grid[1:], ids[1:]):
                step = step * g + idx
            return jnp.minimum(step, last)

        in_specs.append(pl.BlockSpec(
            (None,) * len(job.prefix) + (job.block_rows, job.cols),
            lambda *ids, job=job, block_index=block_index: job.prefix + (block_index(*ids), 0)))
        out_specs.append(pl.BlockSpec(
            (job.block_rows, job.cols), lambda *ids, block_index=block_index: (block_index(*ids), 0)))
        out_shapes.append(jax.ShapeDtypeStruct((job.rows, job.cols), _BF16))
    return in_specs, out_specs, out_shapes


def _run_casts(src_refs, dst_refs):
    for src, dst in zip(src_refs, dst_refs):
        dst[...] = src[...].astype(_BF16)


def _ada_kernel(c_ref, w_ref, b_ref, o_ref):
    cond = _silu(c_ref[...])
    o_ref[0] = jnp.sum(w_ref[0] * cond, axis=0, keepdims=True) + b_ref[0]


def _ada_call(c_col, ada_w, ada_b3):
    return pl.pallas_call(
        _ada_kernel,
        grid=(DEPTH, ADA_WIDTH // ADA_TN),
        in_specs=[
            pl.BlockSpec((D_MODEL, 1), lambda l, j: (0, 0)),
            pl.BlockSpec((1, D_MODEL, ADA_TN), lambda l, j: (l, 0, j)),
            pl.BlockSpec((1, 1, ADA_TN), lambda l, j: (l, 0, j)),
        ],
        out_specs=pl.BlockSpec((1, 1, ADA_TN), lambda l, j: (l, 0, j)),
        out_shape=jax.ShapeDtypeStruct((DEPTH, 1, ADA_WIDTH), _F32),
        compiler_params=pltpu.CompilerParams(
            dimension_semantics=("arbitrary", "arbitrary"),
            vmem_limit_bytes=V7X_VMEM_LIMIT_BYTES),
        name="ada",
    )(c_col, ada_w, ada_b3)


def _ffn_kernel(sub, n_cast, x_ref, ada_ref, pre_g_ref, post_g_ref, wg_ref, wu_ref, wo_ref, *rest):
    cast_src, (o_ref, *cast_dst), (n_scr,) = (
        rest[:n_cast], rest[n_cast:2 * n_cast + 1], rest[2 * n_cast + 1:])
    j = pl.program_id(1)
    last = pl.num_programs(1) - 1

    def matmuls(rows, first):
        n = n_scr[rows, :]
        hg = jnp.dot(n, wg_ref[...], preferred_element_type=_F32)
        hu = jnp.dot(n, wu_ref[...], preferred_element_type=_F32)
        a = (_silu(hg) * hu).astype(_BF16)
        for c in range(D_MODEL // FFN_TN):
            cols = slice(c * FFN_TN, (c + 1) * FFN_TN)
            part = jnp.dot(a, wo_ref[:, cols], preferred_element_type=_F32)
            if first:
                o_ref[rows, cols] = part
            else:
                o_ref[rows, cols] += part

    @pl.when(j == 0)
    def _():
        scale = pre_g_ref[0, sub:sub + 1, :] * (1.0 + _mod(ada_ref, sub, 1))
        shift = _mod(ada_ref, sub, 0)
        for block in _row_blocks(FFN_TM):
            for rows in _row_chunks(block):
                xv = x_ref[rows, :]
                inv = lax.rsqrt(jnp.mean(xv * xv, axis=-1, keepdims=True) + RMS_EPS)
                n_scr[rows, :] = (xv * inv * scale + shift).astype(_BF16)
            matmuls(block, first=True)
        _run_casts(cast_src, cast_dst)

    @pl.when(jnp.logical_and(j > 0, j < last))
    def _():
        matmuls(slice(0, FFN_TM), first=False)
        _run_casts(cast_src, cast_dst)

    @pl.when(j == last)
    def _():
        gain = post_g_ref[0, sub:sub + 1, :] * (0.5 * _mod(ada_ref, sub, 2))
        for block in _row_blocks(FFN_TM):
            matmuls(block, first=False)
            _residual_epilogue(x_ref, o_ref, gain, block)
        _run_casts(cast_src, cast_dst)


def _ffn_call(x, ada, pre_g, post_g, w_in, w_out, layer, which, sub, cast_jobs=()):
    n_f = D_FF // FFN_TF
    grid = (SEQ // FFN_TM, n_f)
    cast_in, cast_out, cast_shapes = _cast_specs(cast_jobs, grid)
    return pl.pallas_call(
        functools.partial(_ffn_kernel, sub, len(cast_jobs)),
        grid=grid,
        in_specs=[
            pl.BlockSpec((FFN_TM, D_MODEL), lambda i, j: (i, 0)),
            pl.BlockSpec((1, 1, ADA_WIDTH), lambda i, j: (layer, 0, 0)),
            pl.BlockSpec((1, N_SUBLAYERS, D_MODEL), lambda i, j: (layer, 0, 0)),
            pl.BlockSpec((1, N_SUBLAYERS, D_MODEL), lambda i, j: (layer, 0, 0)),
            pl.BlockSpec((D_MODEL, FFN_TF), lambda i, j: (0, j)),
            pl.BlockSpec((D_MODEL, FFN_TF), lambda i, j: (0, j + n_f)),
            pl.BlockSpec((FFN_TF, D_MODEL), lambda i, j: (j, 0)),
        ] + cast_in,
        out_specs=[pl.BlockSpec((FFN_TM, D_MODEL), lambda i, j: (i, 0))] + cast_out,
        out_shape=[jax.ShapeDtypeStruct((SEQ, D_MODEL), _F32)] + cast_shapes,
        scratch_shapes=[pltpu.VMEM((FFN_TM, D_MODEL), _BF16)],
        compiler_params=pltpu.CompilerParams(
            dimension_semantics=("arbitrary", "arbitrary"),
            vmem_limit_bytes=V7X_VMEM_LIMIT_BYTES),
        name=f"ffn_l{layer}_{which}",
    )(x, ada, pre_g, post_g, w_in, w_in, w_out, *[job.src for job in cast_jobs])


def _shift_history(buf, halo, tm):
    buf[0:halo, :] = buf[tm:tm + halo, :]


def _mixa_kernel(n_cast, x_ref, ada_ref, pre_g_ref, w_ref, conv_w_ref, conv_b_ref, cln_g_ref,
                 cln_b_ref, pool_w_ref, pool_s_ref, sln_g_ref, sln_b_ref, ws_ref, bs_ref,
                 sconv_w_ref, *rest):
    cast_src, (n_ref, y_ref, *cast_dst), (conv_buf, pool_buf, sconv_buf) = (
        rest[:n_cast], rest[n_cast:2 * n_cast + 2], rest[2 * n_cast + 2:])
    tm = MIXA_TM
    W = BRANCH_WIDTH
    i = pl.program_id(0)

    @pl.when(i == 0)
    def _():
        conv_buf[0:CONV_HALO, :] = jnp.zeros((CONV_HALO, W), _F32)
        pool_buf[0:POOL_HALO, :] = jnp.zeros((POOL_HALO, W), _F32)
        sconv_buf[0:SCONV_HALO, :] = jnp.zeros((SCONV_HALO, W), _F32)

    n = _modulated_norm(x_ref[...], pre_g_ref, ada_ref, 1).astype(_BF16)
    n_ref[...] = n

    p = jnp.dot(n, w_ref[:, 0:2 * W], preferred_element_type=_F32)
    conv_buf[CONV_HALO:CONV_HALO + tm, :] = p[:, 0:W] * _sigmoid(p[:, W:2 * W])
    base = CONV_HALO - (CONV_KERNEL - 1)
    for r0 in range(0, tm, CONV_ROWS):
        acc = None
        for r in range(SUBLANES):
            n_rows = CONV_ROWS if r == 0 else CONV_ROWS + SUBLANES
            part = None
            for k in range(CONV_KERNEL):
                if (base + k) % SUBLANES != r:
                    continue
                start = r0 + (base + k) - r
                term = conv_w_ref[0, k:k + 1, :] * conv_buf[start:start + n_rows, :]
                part = term if part is None else part + term
            if r != 0:
                part = part[r:r + CONV_ROWS, :]
            acc = part if acc is None else acc + part
        acc = acc + conv_b_ref[0]
        y_ref[r0:r0 + CONV_ROWS, 0:W] = _silu(
            _layer_norm(acc, cln_g_ref[0], cln_b_ref[0])).astype(_BF16)
    _shift_history(conv_buf, CONV_HALO, tm)

    p = jnp.dot(n, w_ref[:, 2 * W:3 * W], preferred_element_type=_F32)
    pool_buf[POOL_HALO:POOL_HALO + tm, :] = p
    pos1 = (lax.broadcasted_iota(jnp.int32, (tm, 1), 0) + (i * tm + 1)).astype(_F32)
    for g, win in enumerate(POOL_WINDOWS):
        lanes = slice(g * GROUP_WIDTH, (g + 1) * GROUP_WIDTH)
        tok = p[:, lanes]
        tot = tok
        for d in range(1, win):
            tot = tot + pool_buf[POOL_HALO - d:POOL_HALO - d + tm, lanes]
        pooled = tot / jnp.minimum(pos1, float(win)) - tok
        mixed = jnp.dot(pooled.astype(_BF16), pool_w_ref[0, g].astype(_BF16),
                        preferred_element_type=_F32)
        off = W + g * GROUP_WIDTH
        y_ref[:, off:off + GROUP_WIDTH] = (mixed * pool_s_ref[0, :, lanes]).astype(_BF16)
    _shift_history(pool_buf, POOL_HALO, tm)

    p = jnp.dot(n, w_ref[:, 3 * W:5 * W], preferred_element_type=_F32)
    p = 0.5 * p * (1.0 + lax.erf(p * (2.0 ** -0.5)))
    u = p[:, 0:W]
    v = _layer_norm(p[:, W:2 * W], sln_g_ref[0], sln_b_ref[0]).astype(_BF16)
    n_chunks = tm // SGU_CHUNK
    row = lax.broadcasted_iota(jnp.int32, (SGU_CHUNK, SGU_CHUNK), 0)
    col = lax.broadcasted_iota(jnp.int32, (SGU_CHUNK, SGU_CHUNK), 1)
    for g in range(N_GROUPS):
        lanes = slice(g * GROUP_WIDTH, (g + 1) * GROUP_WIDTH)
        w_causal = jnp.where(col <= row, ws_ref[0, g], 0.0).astype(_BF16)
        vg = jnp.concatenate(
            [v[c * SGU_CHUNK:(c + 1) * SGU_CHUNK, lanes] for c in range(n_chunks)], axis=1)
        sv = jnp.dot(w_causal, vg, preferred_element_type=_F32) + bs_ref[0, :, g:g + 1]
        for c in range(n_chunks):
            rows = slice(c * SGU_CHUNK, (c + 1) * SGU_CHUNK)
            y_ref[rows, 2 * W + g * GROUP_WIDTH:2 * W + (g + 1) * GROUP_WIDTH] = (
                u[rows, lanes] * sv[:, c * GROUP_WIDTH:(c + 1) * GROUP_WIDTH]).astype(_BF16)

    p = jnp.dot(n, w_ref[:, 5 * W:8 * W], preferred_element_type=_F32)
    z = p[:, W:2 * W] * p[:, 2 * W:3 * W]
    sconv_buf[SCONV_HALO:SCONV_HALO + tm, :] = z
    base = SCONV_HALO - (SHORT_CONV_KERNEL - 1)
    acc = sconv_w_ref[0, SHORT_CONV_KERNEL - 1:SHORT_CONV_KERNEL, :] * z
    for k in range(SHORT_CONV_KERNEL - 1):
        acc = acc + sconv_w_ref[0, k:k + 1, :] * sconv_buf[base + k:base + k + tm, :]
    y_ref[:, 3 * W:4 * W] = (p[:, 0:W] * acc).astype(_BF16)
    _shift_history(sconv_buf, SCONV_HALO, tm)
    _run_casts(cast_src, cast_dst)


def _mixa_call(x, ada, pre_g, mix_w_in, conv_w, conv_b, conv_ln_g, conv_ln_b, pool_group_w,
               pool_scale, sgu_ln_g, sgu_ln_b, sgu_w_s, sgu_b_s_t, sconv_w, layer, cast_jobs=()):
    tm, W = MIXA_TM, BRANCH_WIDTH
    grid = (SEQ // tm,)
    cast_in, cast_out, cast_shapes = _cast_specs(cast_jobs, grid)
    row_vec = pl.BlockSpec((1, 1, W), lambda i: (layer, 0, 0))
    return pl.pallas_call(
        functools.partial(_mixa_kernel, len(cast_jobs)),
        grid=grid,
        in_specs=[
            pl.BlockSpec((tm, D_MODEL), lambda i: (i, 0)),
            pl.BlockSpec((1, 1, ADA_WIDTH), lambda i: (layer, 0, 0)),
            pl.BlockSpec((1, N_SUBLAYERS, D_MODEL), lambda i: (layer, 0, 0)),
            pl.BlockSpec((D_MODEL, MIX_IN_WIDTH), lambda i: (0, 0), pipeline_mode=pl.Buffered(1)),
            pl.BlockSpec((1, CONV_KERNEL, W), lambda i: (layer, 0, 0)),
            row_vec, row_vec, row_vec,
            pl.BlockSpec((1, N_GROUPS, GROUP_WIDTH, GROUP_WIDTH), lambda i: (layer, 0, 0, 0)),
            row_vec, row_vec, row_vec,
            pl.BlockSpec((1, N_GROUPS, SGU_CHUNK, SGU_CHUNK), lambda i: (layer, 0, 0, 0)),
            pl.BlockSpec((1, SGU_CHUNK, N_GROUPS), lambda i: (layer, 0, 0)),
            pl.BlockSpec((1, SHORT_CONV_KERNEL, W), lambda i: (layer, 0, 0)),
        ] + cast_in,
        out_specs=[
            pl.BlockSpec((tm, D_MODEL), lambda i: (i, 0)),
            pl.BlockSpec((tm, N_BRANCHES * W), lambda i: (i, 0)),
        ] + cast_out,
        out_shape=[
            jax.ShapeDtypeStruct((SEQ, D_MODEL), _BF16),
            jax.ShapeDtypeStruct((SEQ, N_BRANCHES * W), _BF16),
        ] + cast_shapes,
        scratch_shapes=[
            pltpu.VMEM((CONV_HALO + tm, W), _F32),
            pltpu.VMEM((POOL_HALO + tm, W), _F32),
            pltpu.VMEM((SCONV_HALO + tm, W), _F32),
        ],
        compiler_params=pltpu.CompilerParams(
            dimension_semantics=("arbitrary",),
            vmem_limit_bytes=V7X_VMEM_LIMIT_BYTES),
        name=f"mix_branches_l{layer}",
    )(x, ada, pre_g, mix_w_in, conv_w, conv_b, conv_ln_g, conv_ln_b, pool_group_w, pool_scale,
      sgu_ln_g, sgu_ln_b, sgu_w_s, sgu_b_s_t, sconv_w, *[job.src for job in cast_jobs])


def _mixb_kernel(n_cast, x_ref, n_ref, y_ref, ada_ref, post_g_ref, gw_ref, gb_ref, bw_ref, wo_ref,
                 *rest):
    cast_src, (o_ref, *cast_dst) = rest[:n_cast], rest[n_cast:]
    W = BRANCH_WIDTH
    j = pl.program_id(1)
    last = pl.num_programs(1) - 1

    def matmuls(rows, first):
        n = n_ref[rows, :]
        merged = None
        for b in range(N_BRANCHES):
            gate = _sigmoid(
                jnp.dot(n, gw_ref[b], preferred_element_type=_F32) + gb_ref[0, b:b + 1, :])
            term = gate * jnp.dot(y_ref[rows, b * W:(b + 1) * W], bw_ref[b],
                                  preferred_element_type=_F32)
            merged = term if merged is None else merged + term
        merged = merged.astype(_BF16)
        for c in range(D_MODEL // FFN_TN):
            cols = slice(c * FFN_TN, (c + 1) * FFN_TN)
            part = jnp.dot(merged, wo_ref[:, cols], preferred_element_type=_F32)
            if first:
                o_ref[rows, cols] = part
            else:
                o_ref[rows, cols] += part

    @pl.when(j == 0)
    def _():
        matmuls(slice(0, MIXB_TM), first=True)
        _run_casts(cast_src, cast_dst)

    @pl.when(jnp.logical_and(j > 0, j < last))
    def _():
        matmuls(slice(0, MIXB_TM), first=False)
        _run_casts(cast_src, cast_dst)

    @pl.when(j == last)
    def _():
        gain = post_g_ref[0, 1:2, :] * _mod(ada_ref, 1, 2)
        for block in _row_blocks(MIXB_TM):
            matmuls(block, first=False)
            _residual_epilogue(x_ref, o_ref, gain, block)
        _run_casts(cast_src, cast_dst)


def _mixb_call(x, n, y, ada, post_g, gate_w, gate_b, branch_w_out, w_o, layer, cast_jobs=()):
    tm, tn, W = MIXB_TM, MIXB_TN, BRANCH_WIDTH
    grid = (SEQ // tm, D_MODEL // tn)
    cast_in, cast_out, cast_shapes = _cast_specs(cast_jobs, grid)
    return pl.pallas_call(
        functools.partial(_mixb_kernel, len(cast_jobs)),
        grid=grid,
        in_specs=[
            pl.BlockSpec((tm, D_MODEL), lambda i, j: (i, 0)),
            pl.BlockSpec((tm, D_MODEL), lambda i, j: (i, 0)),
            pl.BlockSpec((tm, N_BRANCHES * W), lambda i, j: (i, 0)),
            pl.BlockSpec((1, 1, ADA_WIDTH), lambda i, j: (layer, 0, 0)),
            pl.BlockSpec((1, N_SUBLAYERS, D_MODEL), lambda i, j: (layer, 0, 0)),
            pl.BlockSpec((N_BRANCHES, D_MODEL, tn), lambda i, j: (0, 0, j)),
            pl.BlockSpec((1, N_BRANCHES, tn), lambda i, j: (layer, 0, j)),
            pl.BlockSpec((N_BRANCHES, W, tn), lambda i, j: (0, 0, j)),
            pl.BlockSpec((tn, D_MODEL), lambda i, j: (j, 0)),
        ] + cast_in,
        out_specs=[pl.BlockSpec((tm, D_MODEL), lambda i, j: (i, 0))] + cast_out,
        out_shape=[jax.ShapeDtypeStruct((SEQ, D_MODEL), _F32)] + cast_shapes,
        compiler_params=pltpu.CompilerParams(
            dimension_semantics=("arbitrary", "arbitrary"),
            vmem_limit_bytes=V7X_VMEM_LIMIT_BYTES),
        name=f"mix_merge_l{layer}",
    )(x, n, y, ada, post_g, gate_w, gate_b, branch_w_out, w_o, *[job.src for job in cast_jobs])


def kernel(x, c, ada_w, ada_b, pre_g, post_g, ffn_w_in, ffn_w_out, mix_w_in, gate_w, gate_b, conv_w, conv_b, conv_ln_g, conv_ln_b, pool_group_w, pool_scale, sgu_ln_g, sgu_ln_b, sgu_w_s, sgu_b_s, sconv_w, branch_w_out, w_o):
    assert x.shape == (1, SEQ, D_MODEL) and c.shape == (1, D_MODEL)
    L, W = DEPTH, BRANCH_WIDTH
    h = x.reshape(SEQ, D_MODEL)
    ada = _ada_call(c.reshape(D_MODEL, 1), ada_w, ada_b.reshape(L, 1, ADA_WIDTH))

    vec = lambda a: a.reshape(L, 1, W)
    sgu_b_s_t = jnp.swapaxes(sgu_b_s, 1, 2)
    gate_w_rows = gate_w.reshape(L, N_BRANCHES * D_MODEL, D_MODEL)
    branch_w_rows = branch_w_out.reshape(L, N_BRANCHES * W, D_MODEL)
    ffn_steps = (SEQ // FFN_TM) * (D_FF // FFN_TF)
    mixa_steps = SEQ // MIXA_TM
    mixb_steps = (SEQ // MIXB_TM) * (D_MODEL // MIXB_TN)

    w_in_b = ffn_w_in[0, 0].astype(_BF16)
    w_out_b = ffn_w_out[0, 0].astype(_BF16)
    for l in range(DEPTH):
        h, mix_in_b, gate_b16, branch_b16, w_o_b = _ffn_call(
            h, ada, pre_g, post_g, w_in_b, w_out_b, l, 0, 0, cast_jobs=(
                _cast_job(mix_w_in, (l,), ffn_steps), _cast_job(gate_w_rows, (l,), ffn_steps),
                _cast_job(branch_w_rows, (l,), ffn_steps), _cast_job(w_o, (l,), ffn_steps)))
        n, y, w_out_b = _mixa_call(
            h, ada, pre_g, mix_in_b, conv_w, vec(conv_b), vec(conv_ln_g), vec(conv_ln_b),
            pool_group_w, vec(pool_scale), vec(sgu_ln_g), vec(sgu_ln_b), sgu_w_s, sgu_b_s_t,
            sconv_w, l, cast_jobs=(_cast_job(ffn_w_out, (l, 1), mixa_steps),))
        h, w_in_b = _mixb_call(
            h, n, y, ada, post_g, gate_b16.reshape(N_BRANCHES, D_MODEL, D_MODEL), gate_b,
            branch_b16.reshape(N_BRANCHES, W, D_MODEL), w_o_b, l,
            cast_jobs=(_cast_job(ffn_w_in, (l, 1), mixb_steps),))
        next_jobs = () if l + 1 == DEPTH else (
            _cast_job(ffn_w_in, (l + 1, 0), ffn_steps), _cast_job(ffn_w_out, (l + 1, 0), ffn_steps))
        h, *next_w = _ffn_call(h, ada, pre_g, post_g, w_in_b, w_out_b, l, 1, 2, cast_jobs=next_jobs)
        if next_w:
            w_in_b, w_out_b = next_w
    return h.reshape(1, SEQ, D_MODEL)
```

```python
import functools
from typing import NamedTuple

import jax
import jax.numpy as jnp
from jax import lax
from jax.experimental import pallas as pl
from jax.experimental.pallas import tpu as pltpu

D_MODEL = 2048
SEQ = 16384
DEPTH = 2
BRANCH_WIDTH = 512
N_BRANCHES = 4
GROUP_WIDTH = 128
N_GROUPS = 4
CONV_KERNEL = 31
POOL_WINDOWS = (2, 4, 8, 16)
SGU_CHUNK = 128
SHORT_CONV_KERNEL = 3
D_FF = 5632
N_SUBLAYERS = 3
N_MOD = 3
ADA_WIDTH = N_SUBLAYERS * N_MOD * D_MODEL
MIX_IN_WIDTH = 8 * BRANCH_WIDTH
RMS_EPS = 1e-6
LN_EPS = 1e-5

V7X_VMEM_LIMIT_BYTES = 60 * 1024 * 1024
SUBLANES = 8
BF16_SUBLANES = 16

ADA_TN = 1024
FFN_TM = 1024
FFN_TF = 512
FFN_TN = 512
ROW_CHUNK = 16
ROW_BLOCK = 256
CONV_ROWS = 64
MIXA_TM = 512
MIXB_TM = 512
MIXB_TN = 512
CONV_HALO = 32
POOL_HALO = 16
SCONV_HALO = 8

_BF16 = jnp.bfloat16
_F32 = jnp.float32


def _sigmoid(v):
    return 1.0 / (1.0 + jnp.exp(-v))


def _silu(v):
    return v * _sigmoid(v)


def _rms_norm(v, g):
    return v * lax.rsqrt(jnp.mean(v * v, axis=-1, keepdims=True) + RMS_EPS) * g


def _layer_norm(v, g, b):
    mu = jnp.mean(v, axis=-1, keepdims=True)
    vc = v - mu
    return vc * lax.rsqrt(jnp.mean(vc * vc, axis=-1, keepdims=True) + LN_EPS) * g + b


def _mod(ada_ref, sub, which):
    off = (sub * N_MOD + which) * D_MODEL
    return ada_ref[0, :, off:off + D_MODEL]


def _modulated_norm(x, pre_g_ref, ada_ref, sub):
    n = _rms_norm(x, pre_g_ref[0, sub:sub + 1, :])
    return n * (1.0 + _mod(ada_ref, sub, 1)) + _mod(ada_ref, sub, 0)


def _row_blocks(tm):
    return [slice(r, r + ROW_BLOCK) for r in range(0, tm, ROW_BLOCK)]


def _row_chunks(block):
    return [slice(r, r + ROW_CHUNK) for r in range(block.start, block.stop, ROW_CHUNK)]


def _residual_epilogue(x_ref, o_ref, gain, block):
    for rows in _row_chunks(block):
        y = o_ref[rows, :]
        inv = lax.rsqrt(jnp.mean(y * y, axis=-1, keepdims=True) + RMS_EPS)
        o_ref[rows, :] = x_ref[rows, :] + y * inv * gain


class _CastJob(NamedTuple):
    src: jax.Array
    prefix: tuple
    block_rows: int

    @property
    def rows(self):
        return self.src.shape[-2]

    @property
    def cols(self):
        return self.src.shape[-1]

    @property
    def n_blocks(self):
        return self.rows // self.block_rows


def _cast_job(src, prefix, n_steps):
    rows = src.shape[-2]
    block_rows = next(b for b in range(BF16_SUBLANES, rows + 1, BF16_SUBLANES)
                      if rows % b == 0 and rows // b <= n_steps)
    return _CastJob(src, tuple(prefix), block_rows)


def _cast_specs(jobs, grid):
    n_steps = 1
    for g in grid:
        n_steps *= g
    in_specs, out_specs, out_shapes = [], [], []
    for job in jobs:
        assert job.rows % job.block_rows == 0 and job.n_blocks <= n_steps
        last = job.n_blocks - 1

        def block_index(*ids, last=last):
            step = ids[0]
            for g, idx in zip(grid[1:], ids[1:]):
                step = step * g + idx
            return jnp.minimum(step, last)

        in_specs.append(pl.BlockSpec(
            (None,) * len(job.prefix) + (job.block_rows, job.cols),
            lambda *ids, job=job, block_index=block_index: job.prefix + (block_index(*ids), 0)))
        out_specs.append(pl.BlockSpec(
            (job.block_rows, job.cols), lambda *ids, block_index=block_index: (block_index(*ids), 0)))
        out_shapes.append(jax.ShapeDtypeStruct((job.rows, job.cols), _BF16))
    return in_specs, out_specs, out_shapes


def _run_casts(src_refs, dst_refs):
    for src, dst in zip(src_refs, dst_refs):
        dst[...] = src[...].astype(_BF16)


def _ada_kernel(c_ref, w_ref, b_ref, o_ref):
    cond = _silu(c_ref[...])
    o_ref[0] = jnp.sum(w_ref[0] * cond, axis=0, keepdims=True) + b_ref[0]


def _ada_call(c_col, ada_w, ada_b3):
    return pl.pallas_call(
        _ada_kernel,
        grid=(DEPTH, ADA_WIDTH // ADA_TN),
        in_specs=[
            pl.BlockSpec((D_MODEL, 1), lambda l, j: (0, 0)),
            pl.BlockSpec((1, D_MODEL, ADA_TN), lambda l, j: (l, 0, j)),
            pl.BlockSpec((1, 1, ADA_TN), lambda l, j: (l, 0, j)),
        ],
        out_specs=pl.BlockSpec((1, 1, ADA_TN), lambda l, j: (l, 0, j)),
        out_shape=jax.ShapeDtypeStruct((DEPTH, 1, ADA_WIDTH), _F32),
        compiler_params=pltpu.CompilerParams(
            dimension_semantics=("arbitrary", "arbitrary"),
            vmem_limit_bytes=V7X_VMEM_LIMIT_BYTES),
        name="ada",
    )(c_col, ada_w, ada_b3)


def _ffn_kernel(sub, n_cast, x_ref, ada_ref, pre_g_ref, post_g_ref, wg_ref, wu_ref, wo_ref, *rest):
    cast_src, (o_ref, *cast_dst), (n_scr,) = (
        rest[:n_cast], rest[n_cast:2 * n_cast + 1], rest[2 * n_cast + 1:])
    j = pl.program_id(1)
    last = pl.num_programs(1) - 1

    def matmuls(rows, first):
        n = n_scr[rows, :]
        hg = jnp.dot(n, wg_ref[...], preferred_element_type=_F32)
        hu = jnp.dot(n, wu_ref[...], preferred_element_type=_F32)
        a = (_silu(hg) * hu).astype(_BF16)
        for c in range(D_MODEL // FFN_TN):
            cols = slice(c * FFN_TN, (c + 1) * FFN_TN)
            part = jnp.dot(a, wo_ref[:, cols], preferred_element_type=_F32)
            if first:
                o_ref[rows, cols] = part
            else:
                o_ref[rows, cols] += part

    @pl.when(j == 0)
    def _():
        scale = pre_g_ref[0, sub:sub + 1, :] * (1.0 + _mod(ada_ref, sub, 1))
        shift = _mod(ada_ref, sub, 0)
        for block in _row_blocks(FFN_TM):
            for rows in _row_chunks(block):
                xv = x_ref[rows, :]
                inv = lax.rsqrt(jnp.mean(xv * xv, axis=-1, keepdims=True) + RMS_EPS)
                n_scr[rows, :] = (xv * inv * scale + shift).astype(_BF16)
            matmuls(block, first=True)
        _run_casts(cast_src, cast_dst)

    @pl.when(jnp.logical_and(j > 0, j < last))
    def _():
        matmuls(slice(0, FFN_TM), first=False)
        _run_casts(cast_src, cast_dst)

    @pl.when(j == last)
    def _():
        gain = post_g_ref[0, sub:sub + 1, :] * (0.5 * _mod(ada_ref, sub, 2))
        for block in _row_blocks(FFN_TM):
            matmuls(block, first=False)
            _residual_epilogue(x_ref, o_ref, gain, block)
        _run_casts(cast_src, cast_dst)


def _ffn_call(x, ada, pre_g, post_g, w_in, w_out, layer, which, sub, cast_jobs=()):
    n_f = D_FF // FFN_TF
    grid = (SEQ // FFN_TM, n_f)
    cast_in, cast_out, cast_shapes = _cast_specs(cast_jobs, grid)
    return pl.pallas_call(
        functools.partial(_ffn_kernel, sub, len(cast_jobs)),
        grid=grid,
        in_specs=[
            pl.BlockSpec((FFN_TM, D_MODEL), lambda i, j: (i, 0)),
            pl.BlockSpec((1, 1, ADA_WIDTH), lambda i, j: (layer, 0, 0)),
            pl.BlockSpec((1, N_SUBLAYERS, D_MODEL), lambda i, j: (layer, 0, 0)),
            pl.BlockSpec((1, N_SUBLAYERS, D_MODEL), lambda i, j: (layer, 0, 0)),
            pl.BlockSpec((D_MODEL, FFN_TF), lambda i, j: (0, j)),
            pl.BlockSpec((D_MODEL, FFN_TF), lambda i, j: (0, j + n_f)),
            pl.BlockSpec((FFN_TF, D_MODEL), lambda i, j: (j, 0)),
        ] + cast_in,
        out_specs=[pl.BlockSpec((FFN_TM, D_MODEL), lambda i, j: (i, 0))] + cast_out,
        out_shape=[jax.ShapeDtypeStruct((SEQ, D_MODEL), _F32)] + cast_shapes,
        scratch_shapes=[pltpu.VMEM((FFN_TM, D_MODEL), _BF16)],
        compiler_params=pltpu.CompilerParams(
            dimension_semantics=("arbitrary", "arbitrary"),
            vmem_limit_bytes=V7X_VMEM_LIMIT_BYTES),
        name=f"ffn_l{layer}_{which}",
    )(x, ada, pre_g, post_g, w_in, w_in, w_out, *[job.src for job in cast_jobs])


def _shift_history(buf, halo, tm):
    buf[0:halo, :] = buf[tm:tm + halo, :]


def _mixa_kernel(layer, n_cast, x_ref, ada_ref, pre_g_ref, w_ref, conv_w_ref, conv_b_ref, cln_g_ref,
                 cln_b_ref, pool_w_ref, pool_s_ref, sln_g_ref, sln_b_ref, ws_ref, bs_ref,
                 sconv_w_ref, *rest):
    cast_src, (n_ref, y_ref, *cast_dst), (conv_buf, pool_buf, sconv_buf) = (
        rest[:n_cast], rest[n_cast:2 * n_cast + 2], rest[2 * n_cast + 2:])
    tm = MIXA_TM
    W = BRANCH_WIDTH
    lrow = slice(layer, layer + 1)
    i = pl.program_id(0)

    @pl.when(i == 0)
    def _():
        conv_buf[0:CONV_HALO, :] = jnp.zeros((CONV_HALO, W), _F32)
        pool_buf[0:POOL_HALO, :] = jnp.zeros((POOL_HALO, W), _F32)
        sconv_buf[0:SCONV_HALO, :] = jnp.zeros((SCONV_HALO, W), _F32)

    n = _modulated_norm(x_ref[...], pre_g_ref, ada_ref, 1).astype(_BF16)
    n_ref[...] = n

    p = jnp.dot(n, w_ref[:, 0:2 * W], preferred_element_type=_F32)
    conv_buf[CONV_HALO:CONV_HALO + tm, :] = p[:, 0:W] * _sigmoid(p[:, W:2 * W])
    base = CONV_HALO - (CONV_KERNEL - 1)
    for r0 in range(0, tm, CONV_ROWS):
        acc = None
        for r in range(SUBLANES):
            n_rows = CONV_ROWS if r == 0 else CONV_ROWS + SUBLANES
            part = None
            for k in range(CONV_KERNEL):
                if (base + k) % SUBLANES != r:
                    continue
                start = r0 + (base + k) - r
                term = conv_w_ref[0, k:k + 1, :] * conv_buf[start:start + n_rows, :]
                part = term if part is None else part + term
            if r != 0:
                part = part[r:r + CONV_ROWS, :]
            acc = part if acc is None else acc + part
        acc = acc + conv_b_ref[lrow, :]
        y_ref[r0:r0 + CONV_ROWS, 0:W] = _silu(
            _layer_norm(acc, cln_g_ref[lrow, :], cln_b_ref[lrow, :])).astype(_BF16)
    _shift_history(conv_buf, CONV_HALO, tm)

    p_pool = jnp.dot(n, w_ref[:, 2 * W:3 * W], preferred_element_type=_F32)
    p_sgu = jnp.dot(n, w_ref[:, 3 * W:5 * W], preferred_element_type=_F32)
    p_sconv = jnp.dot(n, w_ref[:, 5 * W:8 * W], preferred_element_type=_F32)
    p = p_pool
    pool_buf[POOL_HALO:POOL_HALO + tm, :] = p
    pos1 = (lax.broadcasted_iota(jnp.int32, (tm, 1), 0) + (i * tm + 1)).astype(_F32)
    for g, win in enumerate(POOL_WINDOWS):
        lanes = slice(g * GROUP_WIDTH, (g + 1) * GROUP_WIDTH)
        tok = p[:, lanes]
        tot = tok
        for d in range(1, win):
            tot = tot + pool_buf[POOL_HALO - d:POOL_HALO - d + tm, lanes]
        pooled = tot / jnp.minimum(pos1, float(win)) - tok
        mixed = jnp.dot(pooled.astype(_BF16), pool_w_ref[0, g].astype(_BF16),
                        preferred_element_type=_F32)
        off = W + g * GROUP_WIDTH
        y_ref[:, off:off + GROUP_WIDTH] = (mixed * pool_s_ref[lrow, lanes]).astype(_BF16)
    _shift_history(pool_buf, POOL_HALO, tm)

    p = p_sgu
    p = 0.5 * p * (1.0 + lax.erf(p * (2.0 ** -0.5)))
    u = p[:, 0:W]
    v = _layer_norm(p[:, W:2 * W], sln_g_ref[lrow, :], sln_b_ref[lrow, :]).astype(_BF16)
    n_chunks = tm // SGU_CHUNK
    row = lax.broadcasted_iota(jnp.int32, (SGU_CHUNK, SGU_CHUNK), 0)
    col = lax.broadcasted_iota(jnp.int32, (SGU_CHUNK, SGU_CHUNK), 1)
    for g in range(N_GROUPS):
        lanes = slice(g * GROUP_WIDTH, (g + 1) * GROUP_WIDTH)
        w_causal = jnp.where(col <= row, ws_ref[0, g], 0.0).astype(_BF16)
        vg = jnp.concatenate(
            [v[c * SGU_CHUNK:(c + 1) * SGU_CHUNK, lanes] for c in range(n_chunks)], axis=1)
        sv = jnp.dot(w_causal, vg, preferred_element_type=_F32) + bs_ref[0, :, g:g + 1]
        for c in range(n_chunks):
            rows = slice(c * SGU_CHUNK, (c + 1) * SGU_CHUNK)
            y_ref[rows, 2 * W + g * GROUP_WIDTH:2 * W + (g + 1) * GROUP_WIDTH] = (
                u[rows, lanes] * sv[:, c * GROUP_WIDTH:(c + 1) * GROUP_WIDTH]).astype(_BF16)

    p = p_sconv
    z = p[:, W:2 * W] * p[:, 2 * W:3 * W]
    sconv_buf[SCONV_HALO:SCONV_HALO + tm, :] = z
    base = SCONV_HALO - (SHORT_CONV_KERNEL - 1)
    acc = sconv_w_ref[0, SHORT_CONV_KERNEL - 1:SHORT_CONV_KERNEL, :] * z
    for k in range(SHORT_CONV_KERNEL - 1):
        acc = acc + sconv_w_ref[0, k:k + 1, :] * sconv_buf[base + k:base + k + tm, :]
    y_ref[:, 3 * W:4 * W] = (p[:, 0:W] * acc).astype(_BF16)
    _shift_history(sconv_buf, SCONV_HALO, tm)
    _run_casts(cast_src, cast_dst)


def _mixa_call(x, ada, pre_g, mix_w_in, conv_w, conv_b, conv_ln_g, conv_ln_b, pool_group_w,
               pool_scale, sgu_ln_g, sgu_ln_b, sgu_w_s, sgu_b_s_t, sconv_w, layer, cast_jobs=()):
    tm, W = MIXA_TM, BRANCH_WIDTH
    grid = (SEQ // tm,)
    cast_in, cast_out, cast_shapes = _cast_specs(cast_jobs, grid)
    row_vec = pl.BlockSpec((DEPTH, W), lambda i: (0, 0))
    return pl.pallas_call(
        functools.partial(_mixa_kernel, layer, len(cast_jobs)),
        grid=grid,
        in_specs=[
            pl.BlockSpec((tm, D_MODEL), lambda i: (i, 0)),
            pl.BlockSpec((1, 1, ADA_WIDTH), lambda i: (layer, 0, 0)),
            pl.BlockSpec((1, N_SUBLAYERS, D_MODEL), lambda i: (layer, 0, 0)),
            pl.BlockSpec((D_MODEL, MIX_IN_WIDTH), lambda i: (0, 0), pipeline_mode=pl.Buffered(1)),
            pl.BlockSpec((1, CONV_KERNEL, W), lambda i: (layer, 0, 0)),
            row_vec, row_vec, row_vec,
            pl.BlockSpec((1, N_GROUPS, GROUP_WIDTH, GROUP_WIDTH), lambda i: (layer, 0, 0, 0)),
            row_vec, row_vec, row_vec,
            pl.BlockSpec((1, N_GROUPS, SGU_CHUNK, SGU_CHUNK), lambda i: (layer, 0, 0, 0)),
            pl.BlockSpec((1, SGU_CHUNK, N_GROUPS), lambda i: (layer, 0, 0)),
            pl.BlockSpec((1, SHORT_CONV_KERNEL, W), lambda i: (layer, 0, 0)),
        ] + cast_in,
        out_specs=[
            pl.BlockSpec((tm, D_MODEL), lambda i: (i, 0)),
            pl.BlockSpec((tm, N_BRANCHES * W), lambda i: (i, 0)),
        ] + cast_out,
        out_shape=[
            jax.ShapeDtypeStruct((SEQ, D_MODEL), _BF16),
            jax.ShapeDtypeStruct((SEQ, N_BRANCHES * W), _BF16),
        ] + cast_shapes,
        scratch_shapes=[
            pltpu.VMEM((CONV_HALO + tm, W), _F32),
            pltpu.VMEM((POOL_HALO + tm, W), _F32),
            pltpu.VMEM((SCONV_HALO + tm, W), _F32),
        ],
        compiler_params=pltpu.CompilerParams(
            dimension_semantics=("arbitrary",),
            vmem_limit_bytes=V7X_VMEM_LIMIT_BYTES),
        name=f"mix_branches_l{layer}",
    )(x, ada, pre_g, mix_w_in, conv_w, conv_b, conv_ln_g, conv_ln_b, pool_group_w, pool_scale,
      sgu_ln_g, sgu_ln_b, sgu_w_s, sgu_b_s_t, sconv_w, *[job.src for job in cast_jobs])


def _mixb_kernel(n_cast, x_ref, n_ref, y_ref, ada_ref, post_g_ref, gw_ref, gb_ref, bw_ref, wo_ref,
                 *rest):
    cast_src, (o_ref, *cast_dst) = rest[:n_cast], rest[n_cast:]
    W = BRANCH_WIDTH
    j = pl.program_id(1)
    last = pl.num_programs(1) - 1

    def matmuls(rows, first):
        n = n_ref[rows, :]
        merged = None
        for b in range(N_BRANCHES):
            gate = _sigmoid(
                jnp.dot(n, gw_ref[b], preferred_element_type=_F32) + gb_ref[0, b:b + 1, :])
            term = gate * jnp.dot(y_ref[rows, b * W:(b + 1) * W], bw_ref[b],
                                  preferred_element_type=_F32)
            merged = term if merged is None else merged + term
        merged = merged.astype(_BF16)
        for c in range(D_MODEL // FFN_TN):
            cols = slice(c * FFN_TN, (c + 1) * FFN_TN)
            part = jnp.dot(merged, wo_ref[:, cols], preferred_element_type=_F32)
            if first:
                o_ref[rows, cols] = part
            else:
                o_ref[rows, cols] += part

    @pl.when(j == 0)
    def _():
        matmuls(slice(0, MIXB_TM), first=True)
        _run_casts(cast_src, cast_dst)

    @pl.when(jnp.logical_and(j > 0, j < last))
    def _():
        matmuls(slice(0, MIXB_TM), first=False)
        _run_casts(cast_src, cast_dst)

    @pl.when(j == last)
    def _():
        gain = post_g_ref[0, 1:2, :] * _mod(ada_ref, 1, 2)
        for block in _row_blocks(MIXB_TM):
            matmuls(block, first=False)
            _residual_epilogue(x_ref, o_ref, gain, block)
        _run_casts(cast_src, cast_dst)


def _mixb_call(x, n, y, ada, post_g, gate_w, gate_b, branch_w_out, w_o, layer, cast_jobs=()):
    tm, tn, W = MIXB_TM, MIXB_TN, BRANCH_WIDTH
    grid = (SEQ // tm, D_MODEL // tn)
    cast_in, cast_out, cast_shapes = _cast_specs(cast_jobs, grid)
    return pl.pallas_call(
        functools.partial(_mixb_kernel, len(cast_jobs)),
        grid=grid,
        in_specs=[
            pl.BlockSpec((tm, D_MODEL), lambda i, j: (i, 0)),
            pl.BlockSpec((tm, D_MODEL), lambda i, j: (i, 0)),
            pl.BlockSpec((tm, N_BRANCHES * W), lambda i, j: (i, 0)),
            pl.BlockSpec((1, 1, ADA_WIDTH), lambda i, j: (layer, 0, 0)),
            pl.BlockSpec((1, N_SUBLAYERS, D_MODEL), lambda i, j: (layer, 0, 0)),
            pl.BlockSpec((N_BRANCHES, D_MODEL, tn), lambda i, j: (0, 0, j)),
            pl.BlockSpec((1, N_BRANCHES, tn), lambda i, j: (layer, 0, j)),
            pl.BlockSpec((N_BRANCHES, W, tn), lambda i, j: (0, 0, j)),
            pl.BlockSpec((tn, D_MODEL), lambda i, j: (j, 0)),
        ] + cast_in,
        out_specs=[pl.BlockSpec((tm, D_MODEL), lambda i, j: (i, 0))] + cast_out,
        out_shape=[jax.ShapeDtypeStruct((SEQ, D_MODEL), _F32)] + cast_shapes,
        compiler_params=pltpu.CompilerParams(
            dimension_semantics=("arbitrary", "arbitrary"),
            vmem_limit_bytes=V7X_VMEM_LIMIT_BYTES),
        name=f"mix_merge_l{layer}",
    )(x, n, y, ada, post_g, gate_w, gate_b, branch_w_out, w_o, *[job.src for job in cast_jobs])


def kernel(x, c, ada_w, ada_b, pre_g, post_g, ffn_w_in, ffn_w_out, mix_w_in, gate_w, gate_b, conv_w, conv_b, conv_ln_g, conv_ln_b, pool_group_w, pool_scale, sgu_ln_g, sgu_ln_b, sgu_w_s, sgu_b_s, sconv_w, branch_w_out, w_o):
    assert x.shape == (1, SEQ, D_MODEL) and c.shape == (1, D_MODEL)
    L, W = DEPTH, BRANCH_WIDTH
    h = x.reshape(SEQ, D_MODEL)
    ada = _ada_call(c.reshape(D_MODEL, 1), ada_w, ada_b.reshape(L, 1, ADA_WIDTH))

    sgu_b_s_t = jnp.swapaxes(sgu_b_s, 1, 2)
    gate_w_rows = gate_w.reshape(L, N_BRANCHES * D_MODEL, D_MODEL)
    branch_w_rows = branch_w_out.reshape(L, N_BRANCHES * W, D_MODEL)
    ffn_steps = (SEQ // FFN_TM) * (D_FF // FFN_TF)
    mixa_steps = SEQ // MIXA_TM
    mixb_steps = (SEQ // MIXB_TM) * (D_MODEL // MIXB_TN)

    w_in_b = ffn_w_in[0, 0].astype(_BF16)
    w_out_b = ffn_w_out[0, 0].astype(_BF16)
    for l in range(DEPTH):
        h, mix_in_b, gate_b16, branch_b16, w_o_b = _ffn_call(
            h, ada, pre_g, post_g, w_in_b, w_out_b, l, 0, 0, cast_jobs=(
                _cast_job(mix_w_in, (l,), ffn_steps), _cast_job(gate_w_rows, (l,), ffn_steps),
                _cast_job(branch_w_rows, (l,), ffn_steps), _cast_job(w_o, (l,), ffn_steps)))
        n, y, w_out_b = _mixa_call(
            h, ada, pre_g, mix_in_b, conv_w, conv_b, conv_ln_g, conv_ln_b,
            pool_group_w, pool_scale, sgu_ln_g, sgu_ln_b, sgu_w_s, sgu_b_s_t,
            sconv_w, l, cast_jobs=(_cast_job(ffn_w_out, (l, 1), mixa_steps),))
        h, w_in_b = _mixb_call(
            h, n, y, ada, post_g, gate_b16.reshape(N_BRANCHES, D_MODEL, D_MODEL), gate_b,
            branch_b16.reshape(N_BRANCHES, W, D_MODEL), w_o_b, l,
            cast_jobs=(_cast_job(ffn_w_in, (l, 1), mixb_steps),))
        next_jobs = () if l + 1 == DEPTH else (
            _cast_job(ffn_w_in, (l + 1, 0), ffn_steps), _cast_job(ffn_w_out, (l + 1, 0), ffn_steps))
        h, *next_w = _ffn_call(h, ada, pre_g, post_g, w_in_b, w_out_b, l, 1, 2, cast_jobs=next_jobs)
        if next_w:
            w_in_b, w_out_b = next_w
    return h.reshape(1, SEQ, D_MODEL)
```

```python
import functools
from typing import NamedTuple

import jax
import jax.numpy as jnp
from jax import lax
from jax.experimental import pallas as pl
from jax.experimental.pallas import tpu as pltpu

D_MODEL = 2048
SEQ = 16384
DEPTH = 2
BRANCH_WIDTH = 512
N_BRANCHES = 4
GROUP_WIDTH = 128
N_GROUPS = 4
CONV_KERNEL = 31
POOL_WINDOWS = (2, 4, 8, 16)
SGU_CHUNK = 128
SHORT_CONV_KERNEL = 3
D_FF = 5632
N_SUBLAYERS = 3
N_MOD = 3
ADA_WIDTH = N_SUBLAYERS * N_MOD * D_MODEL
MIX_IN_WIDTH = 8 * BRANCH_WIDTH
RMS_EPS = 1e-6
LN_EPS = 1e-5

V7X_VMEM_LIMIT_BYTES = 60 * 1024 * 1024
SUBLANES = 8
BF16_SUBLANES = 16

ADA_TN = 1024
FFN_TM = 1024
FFN_TF = 512
FFN_TN = 512
ROW_CHUNK = 16
FFN_ROW_BLOCK = 512
MIXB_ROW_BLOCK = 256
CONV_ROWS = 64
MIXA_TM = 512
MIXB_TM = 512
MIXB_TN = 512
CONV_HALO = 32
POOL_HALO = 16
SCONV_HALO = 8

_BF16 = jnp.bfloat16
_F32 = jnp.float32


def _sigmoid(v):
    return 1.0 / (1.0 + jnp.exp(-v))


def _silu(v):
    return v * _sigmoid(v)


def _rms_norm(v, g):
    return v * lax.rsqrt(jnp.mean(v * v, axis=-1, keepdims=True) + RMS_EPS) * g


def _layer_norm(v, g, b):
    mu = jnp.mean(v, axis=-1, keepdims=True)
    vc = v - mu
    return vc * lax.rsqrt(jnp.mean(vc * vc, axis=-1, keepdims=True) + LN_EPS) * g + b


def _mod(ada_ref, sub, which):
    off = (sub * N_MOD + which) * D_MODEL
    return ada_ref[0, :, off:off + D_MODEL]


def _modulated_norm(x, pre_g_ref, ada_ref, sub):
    n = _rms_norm(x, pre_g_ref[0, sub:sub + 1, :])
    return n * (1.0 + _mod(ada_ref, sub, 1)) + _mod(ada_ref, sub, 0)


def _row_blocks(tm, block_rows):
    return [slice(r, r + block_rows) for r in range(0, tm, block_rows)]


def _row_chunks(block):
    return [slice(r, r + ROW_CHUNK) for r in range(block.start, block.stop, ROW_CHUNK)]


def _residual_epilogue(x_ref, o_ref, gain, block):
    for rows in _row_chunks(block):
        y = o_ref[rows, :]
        inv = lax.rsqrt(jnp.mean(y * y, axis=-1, keepdims=True) + RMS_EPS)
        o_ref[rows, :] = x_ref[rows, :] + y * inv * gain


class _CastJob(NamedTuple):
    src: jax.Array
    prefix: tuple
    block_rows: int

    @property
    def rows(self):
        return self.src.shape[-2]

    @property
    def cols(self):
        return self.src.shape[-1]

    @property
    def n_blocks(self):
        return self.rows // self.block_rows


def _cast_job(src, prefix, n_steps):
    rows = src.shape[-2]
    block_rows = next(b for b in range(BF16_SUBLANES, rows + 1, BF16_SUBLANES)
                      if rows % b == 0 and rows // b <= n_steps)
    return _CastJob(src, tuple(prefix), block_rows)


def _cast_specs(jobs, grid):
    n_steps = 1
    for g in grid:
        n_steps *= g
    in_specs, out_specs, out_shapes = [], [], []
    for job in jobs:
        assert job.rows % job.block_rows == 0 and job.n_blocks <= n_steps
        last = job.n_blocks - 1

        def block_index(*ids, last=last):
            step = ids[0]
            for g, idx in zip(grid[1:], ids[1:]):
                step = step * g + idx
            return jnp.minimum(step, last)

        in_specs.append(pl.BlockSpec(
            (None,) * len(job.prefix) + (job.block_rows, job.cols),
            lambda *ids, job=job, block_index=block_index: job.prefix + (block_index(*ids), 0)))
        out_specs.append(pl.BlockSpec(
            (job.block_rows, job.cols), lambda *ids, block_index=block_index: (block_index(*ids), 0)))
        out_shapes.append(jax.ShapeDtypeStruct((job.rows, job.cols), _BF16))
    return in_specs, out_specs, out_shapes


def _run_casts(src_refs, dst_refs):
    for src, dst in zip(src_refs, dst_refs):
        dst[...] = src[...].astype(_BF16)


def _ada_kernel(c_ref, w_ref, b_ref, o_ref):
    cond = _silu(c_ref[...])
    o_ref[0] = jnp.sum(w_ref[0] * cond, axis=0, keepdims=True) + b_ref[0]


def _ada_call(c_col, ada_w, ada_b3):
    return pl.pallas_call(
        _ada_kernel,
        grid=(DEPTH, ADA_WIDTH // ADA_TN),
        in_specs=[
            pl.BlockSpec((D_MODEL, 1), lambda l, j: (0, 0)),
            pl.BlockSpec((1, D_MODEL, ADA_TN), lambda l, j: (l, 0, j)),
            pl.BlockSpec((1, 1, ADA_TN), lambda l, j: (l, 0, j)),
        ],
        out_specs=pl.BlockSpec((1, 1, ADA_TN), lambda l, j: (l, 0, j)),
        out_shape=jax.ShapeDtypeStruct((DEPTH, 1, ADA_WIDTH), _F32),
        compiler_params=pltpu.CompilerParams(
            dimension_semantics=("arbitrary", "arbitrary"),
            vmem_limit_bytes=V7X_VMEM_LIMIT_BYTES),
        name="ada",
    )(c_col, ada_w, ada_b3)


def _ffn_kernel(sub, n_cast, x_ref, ada_ref, pre_g_ref, post_g_ref, wg_ref, wu_ref, wo_ref, *rest):
    cast_src, (o_ref, *cast_dst), (n_scr,) = (
        rest[:n_cast], rest[n_cast:2 * n_cast + 1], rest[2 * n_cast + 1:])
    j = pl.program_id(1)
    last = pl.num_programs(1) - 1

    def matmuls(rows, first):
        n = n_scr[rows, :]
        hg = jnp.dot(n, wg_ref[...], preferred_element_type=_F32)
        hu = jnp.dot(n, wu_ref[...], preferred_element_type=_F32)
        a = (_silu(hg) * hu).astype(_BF16)
        for c in range(D_MODEL // FFN_TN):
            cols = slice(c * FFN_TN, (c + 1) * FFN_TN)
            part = jnp.dot(a, wo_ref[:, cols], preferred_element_type=_F32)
            if first:
                o_ref[rows, cols] = part
            else:
                o_ref[rows, cols] += part

    @pl.when(j == 0)
    def _():
        scale = pre_g_ref[0, sub:sub + 1, :] * (1.0 + _mod(ada_ref, sub, 1))
        shift = _mod(ada_ref, sub, 0)
        for block in _row_blocks(FFN_TM, FFN_ROW_BLOCK):
            for rows in _row_chunks(block):
                xv = x_ref[rows, :]
                inv = lax.rsqrt(jnp.mean(xv * xv, axis=-1, keepdims=True) + RMS_EPS)
                n_scr[rows, :] = (xv * inv * scale + shift).astype(_BF16)
            matmuls(block, first=True)
        _run_casts(cast_src, cast_dst)

    @pl.when(jnp.logical_and(j > 0, j < last))
    def _():
        matmuls(slice(0, FFN_TM), first=False)
        _run_casts(cast_src, cast_dst)

    @pl.when(j == last)
    def _():
        gain = post_g_ref[0, sub:sub + 1, :] * (0.5 * _mod(ada_ref, sub, 2))
        for block in _row_blocks(FFN_TM, FFN_ROW_BLOCK):
            matmuls(block, first=False)
            _residual_epilogue(x_ref, o_ref, gain, block)
        _run_casts(cast_src, cast_dst)


def _ffn_call(x, ada, pre_g, post_g, w_in, w_out, layer, which, sub, cast_jobs=()):
    n_f = D_FF // FFN_TF
    grid = (SEQ // FFN_TM, n_f)
    cast_in, cast_out, cast_shapes = _cast_specs(cast_jobs, grid)
    return pl.pallas_call(
        functools.partial(_ffn_kernel, sub, len(cast_jobs)),
        grid=grid,
        in_specs=[
            pl.BlockSpec((FFN_TM, D_MODEL), lambda i, j: (i, 0)),
            pl.BlockSpec((1, 1, ADA_WIDTH), lambda i, j: (layer, 0, 0)),
            pl.BlockSpec((1, N_SUBLAYERS, D_MODEL), lambda i, j: (layer, 0, 0)),
            pl.BlockSpec((1, N_SUBLAYERS, D_MODEL), lambda i, j: (layer, 0, 0)),
            pl.BlockSpec((D_MODEL, FFN_TF), lambda i, j: (0, j)),
            pl.BlockSpec((D_MODEL, FFN_TF), lambda i, j: (0, j + n_f)),
            pl.BlockSpec((FFN_TF, D_MODEL), lambda i, j: (j, 0)),
        ] + cast_in,
        out_specs=[pl.BlockSpec((FFN_TM, D_MODEL), lambda i, j: (i, 0))] + cast_out,
        out_shape=[jax.ShapeDtypeStruct((SEQ, D_MODEL), _F32)] + cast_shapes,
        scratch_shapes=[pltpu.VMEM((FFN_TM, D_MODEL), _BF16)],
        compiler_params=pltpu.CompilerParams(
            dimension_semantics=("arbitrary", "arbitrary"),
            vmem_limit_bytes=V7X_VMEM_LIMIT_BYTES),
        name=f"ffn_l{layer}_{which}",
    )(x, ada, pre_g, post_g, w_in, w_in, w_out, *[job.src for job in cast_jobs])


def _shift_history(buf, halo, tm):
    buf[0:halo, :] = buf[tm:tm + halo, :]


def _mixa_kernel(layer, n_cast, x_ref, ada_ref, pre_g_ref, w_ref, conv_w_ref, conv_b_ref, cln_g_ref,
                 cln_b_ref, pool_w_ref, pool_s_ref, sln_g_ref, sln_b_ref, ws_ref, bs_ref,
                 sconv_w_ref, *rest):
    cast_src, (n_ref, y_ref, *cast_dst), (conv_buf, pool_buf, sconv_buf) = (
        rest[:n_cast], rest[n_cast:2 * n_cast + 2], rest[2 * n_cast + 2:])
    tm = MIXA_TM
    W = BRANCH_WIDTH
    lrow = slice(layer, layer + 1)
    i = pl.program_id(0)

    @pl.when(i == 0)
    def _():
        conv_buf[0:CONV_HALO, :] = jnp.zeros((CONV_HALO, W), _F32)
        pool_buf[0:POOL_HALO, :] = jnp.zeros((POOL_HALO, W), _F32)
        sconv_buf[0:SCONV_HALO, :] = jnp.zeros((SCONV_HALO, W), _F32)

    n = _modulated_norm(x_ref[...], pre_g_ref, ada_ref, 1).astype(_BF16)
    n_ref[...] = n

    p = jnp.dot(n, w_ref[:, 0:2 * W], preferred_element_type=_F32)
    conv_buf[CONV_HALO:CONV_HALO + tm, :] = p[:, 0:W] * _sigmoid(p[:, W:2 * W])
    base = CONV_HALO - (CONV_KERNEL - 1)
    for r0 in range(0, tm, CONV_ROWS):
        acc = None
        for r in range(SUBLANES):
            n_rows = CONV_ROWS if r == 0 else CONV_ROWS + SUBLANES
            part = None
            for k in range(CONV_KERNEL):
                if (base + k) % SUBLANES != r:
                    continue
                start = r0 + (base + k) - r
                term = conv_w_ref[0, k:k + 1, :] * conv_buf[start:start + n_rows, :]
                part = term if part is None else part + term
            if r != 0:
                part = part[r:r + CONV_ROWS, :]
            acc = part if acc is None else acc + part
        acc = acc + conv_b_ref[lrow, :]
        y_ref[r0:r0 + CONV_ROWS, 0:W] = _silu(
            _layer_norm(acc, cln_g_ref[lrow, :], cln_b_ref[lrow, :])).astype(_BF16)
    _shift_history(conv_buf, CONV_HALO, tm)

    p_pool = jnp.dot(n, w_ref[:, 2 * W:3 * W], preferred_element_type=_F32)
    p_sgu = jnp.dot(n, w_ref[:, 3 * W:5 * W], preferred_element_type=_F32)
    p_sconv = jnp.dot(n, w_ref[:, 5 * W:8 * W], preferred_element_type=_F32)
    p = p_pool
    pool_buf[POOL_HALO:POOL_HALO + tm, :] = p
    pos1 = (lax.broadcasted_iota(jnp.int32, (tm, 1), 0) + (i * tm + 1)).astype(_F32)
    for g, win in enumerate(POOL_WINDOWS):
        lanes = slice(g * GROUP_WIDTH, (g + 1) * GROUP_WIDTH)
        tok = p[:, lanes]
        tot = tok
        for d in range(1, win):
            tot = tot + pool_buf[POOL_HALO - d:POOL_HALO - d + tm, lanes]
        pooled = tot / jnp.minimum(pos1, float(win)) - tok
        mixed = jnp.dot(pooled.astype(_BF16), pool_w_ref[0, g].astype(_BF16),
                        preferred_element_type=_F32)
        off = W + g * GROUP_WIDTH
        y_ref[:, off:off + GROUP_WIDTH] = (mixed * pool_s_ref[lrow, lanes]).astype(_BF16)
    _shift_history(pool_buf, POOL_HALO, tm)

    p = p_sgu
    p = 0.5 * p * (1.0 + lax.erf(p * (2.0 ** -0.5)))
    u = p[:, 0:W]
    v = _layer_norm(p[:, W:2 * W], sln_g_ref[lrow, :], sln_b_ref[lrow, :]).astype(_BF16)
    n_chunks = tm // SGU_CHUNK
    row = lax.broadcasted_iota(jnp.int32, (SGU_CHUNK, SGU_CHUNK), 0)
    col = lax.broadcasted_iota(jnp.int32, (SGU_CHUNK, SGU_CHUNK), 1)
    for g in range(N_GROUPS):
        lanes = slice(g * GROUP_WIDTH, (g + 1) * GROUP_WIDTH)
        w_causal = jnp.where(col <= row, ws_ref[0, g], 0.0).astype(_BF16)
        vg = jnp.concatenate(
            [v[c * SGU_CHUNK:(c + 1) * SGU_CHUNK, lanes] for c in range(n_chunks)], axis=1)
        sv = jnp.dot(w_causal, vg, preferred_element_type=_F32) + bs_ref[0, :, g:g + 1]
        for c in range(n_chunks):
            rows = slice(c * SGU_CHUNK, (c + 1) * SGU_CHUNK)
            y_ref[rows, 2 * W + g * GROUP_WIDTH:2 * W + (g + 1) * GROUP_WIDTH] = (
                u[rows, lanes] * sv[:, c * GROUP_WIDTH:(c + 1) * GROUP_WIDTH]).astype(_BF16)

    p = p_sconv
    z = p[:, W:2 * W] * p[:, 2 * W:3 * W]
    sconv_buf[SCONV_HALO:SCONV_HALO + tm, :] = z
    base = SCONV_HALO - (SHORT_CONV_KERNEL - 1)
    acc = sconv_w_ref[0, SHORT_CONV_KERNEL - 1:SHORT_CONV_KERNEL, :] * z
    for k in range(SHORT_CONV_KERNEL - 1):
        acc = acc + sconv_w_ref[0, k:k + 1, :] * sconv_buf[base + k:base + k + tm, :]
    y_ref[:, 3 * W:4 * W] = (p[:, 0:W] * acc).astype(_BF16)
    _shift_history(sconv_buf, SCONV_HALO, tm)
    _run_casts(cast_src, cast_dst)


def _mixa_call(x, ada, pre_g, mix_w_in, conv_w, conv_b, conv_ln_g, conv_ln_b, pool_group_w,
               pool_scale, sgu_ln_g, sgu_ln_b, sgu_w_s, sgu_b_s_t, sconv_w, layer, cast_jobs=()):
    tm, W = MIXA_TM, BRANCH_WIDTH
    grid = (SEQ // tm,)
    cast_in, cast_out, cast_shapes = _cast_specs(cast_jobs, grid)
    row_vec = pl.BlockSpec((DEPTH, W), lambda i: (0, 0))
    return pl.pallas_call(
        functools.partial(_mixa_kernel, layer, len(cast_jobs)),
        grid=grid,
        in_specs=[
            pl.BlockSpec((tm, D_MODEL), lambda i: (i, 0)),
            pl.BlockSpec((1, 1, ADA_WIDTH), lambda i: (layer, 0, 0)),
            pl.BlockSpec((1, N_SUBLAYERS, D_MODEL), lambda i: (layer, 0, 0)),
            pl.BlockSpec((D_MODEL, MIX_IN_WIDTH), lambda i: (0, 0), pipeline_mode=pl.Buffered(1)),
            pl.BlockSpec((1, CONV_KERNEL, W), lambda i: (layer, 0, 0)),
            row_vec, row_vec, row_vec,
            pl.BlockSpec((1, N_GROUPS, GROUP_WIDTH, GROUP_WIDTH), lambda i: (layer, 0, 0, 0)),
            row_vec, row_vec, row_vec,
            pl.BlockSpec((1, N_GROUPS, SGU_CHUNK, SGU_CHUNK), lambda i: (layer, 0, 0, 0)),
            pl.BlockSpec((1, SGU_CHUNK, N_GROUPS), lambda i: (layer, 0, 0)),
            pl.BlockSpec((1, SHORT_CONV_KERNEL, W), lambda i: (layer, 0, 0)),
        ] + cast_in,
        out_specs=[
            pl.BlockSpec((tm, D_MODEL), lambda i: (i, 0)),
            pl.BlockSpec((tm, N_BRANCHES * W), lambda i: (i, 0)),
        ] + cast_out,
        out_shape=[
            jax.ShapeDtypeStruct((SEQ, D_MODEL), _BF16),
            jax.ShapeDtypeStruct((SEQ, N_BRANCHES * W), _BF16),
        ] + cast_shapes,
        scratch_shapes=[
            pltpu.VMEM((CONV_HALO + tm, W), _F32),
            pltpu.VMEM((POOL_HALO + tm, W), _F32),
            pltpu.VMEM((SCONV_HALO + tm, W), _F32),
        ],
        compiler_params=pltpu.CompilerParams(
            dimension_semantics=("arbitrary",),
            vmem_limit_bytes=V7X_VMEM_LIMIT_BYTES),
        name=f"mix_branches_l{layer}",
    )(x, ada, pre_g, mix_w_in, conv_w, conv_b, conv_ln_g, conv_ln_b, pool_group_w, pool_scale,
      sgu_ln_g, sgu_ln_b, sgu_w_s, sgu_b_s_t, sconv_w, *[job.src for job in cast_jobs])


def _mixb_kernel(n_cast, x_ref, n_ref, y_ref, ada_ref, post_g_ref, gw_ref, gb_ref, bw_ref, wo_ref,
                 *rest):
    cast_src, (o_ref, *cast_dst) = rest[:n_cast], rest[n_cast:]
    W = BRANCH_WIDTH
    j = pl.program_id(1)
    last = pl.num_programs(1) - 1

    def matmuls(rows, first):
        n = n_ref[rows, :]
        merged = None
        for b in range(N_BRANCHES):
            gate = _sigmoid(
                jnp.dot(n, gw_ref[b], preferred_element_type=_F32) + gb_ref[0, b:b + 1, :])
            term = gate * jnp.dot(y_ref[rows, b * W:(b + 1) * W], bw_ref[b],
                                  preferred_element_type=_F32)
            merged = term if merged is None else merged + term
        merged = merged.astype(_BF16)
        for c in range(D_MODEL // FFN_TN):
            cols = slice(c * FFN_TN, (c + 1) * FFN_TN)
            part = jnp.dot(merged, wo_ref[:, cols], preferred_element_type=_F32)
            if first:
                o_ref[rows, cols] = part
            else:
                o_ref[rows, cols] += part

    @pl.when(j == 0)
    def _():
        matmuls(slice(0, MIXB_TM), first=True)
        _run_casts(cast_src, cast_dst)

    @pl.when(jnp.logical_and(j > 0, j < last))
    def _():
        matmuls(slice(0, MIXB_TM), first=False)
        _run_casts(cast_src, cast_dst)

    @pl.when(j == last)
    def _():
        gain = post_g_ref[0, 1:2, :] * _mod(ada_ref, 1, 2)
        for block in _row_blocks(MIXB_TM, MIXB_ROW_BLOCK):
            matmuls(block, first=False)
            _residual_epilogue(x_ref, o_ref, gain, block)
        _run_casts(cast_src, cast_dst)


def _mixb_call(x, n, y, ada, post_g, gate_w, gate_b, branch_w_out, w_o, layer, cast_jobs=()):
    tm, tn, W = MIXB_TM, MIXB_TN, BRANCH_WIDTH
    grid = (SEQ // tm, D_MODEL // tn)
    cast_in, cast_out, cast_shapes = _cast_specs(cast_jobs, grid)
    return pl.pallas_call(
        functools.partial(_mixb_kernel, len(cast_jobs)),
        grid=grid,
        in_specs=[
            pl.BlockSpec((tm, D_MODEL), lambda i, j: (i, 0)),
            pl.BlockSpec((tm, D_MODEL), lambda i, j: (i, 0)),
            pl.BlockSpec((tm, N_BRANCHES * W), lambda i, j: (i, 0)),
            pl.BlockSpec((1, 1, ADA_WIDTH), lambda i, j: (layer, 0, 0)),
            pl.BlockSpec((1, N_SUBLAYERS, D_MODEL), lambda i, j: (layer, 0, 0)),
            pl.BlockSpec((N_BRANCHES, D_MODEL, tn), lambda i, j: (0, 0, j)),
            pl.BlockSpec((1, N_BRANCHES, tn), lambda i, j: (layer, 0, j)),
            pl.BlockSpec((N_BRANCHES, W, tn), lambda i, j: (0, 0, j)),
            pl.BlockSpec((tn, D_MODEL), lambda i, j: (j, 0)),
        ] + cast_in,
        out_specs=[pl.BlockSpec((tm, D_MODEL), lambda i, j: (i, 0))] + cast_out,
        out_shape=[jax.ShapeDtypeStruct((SEQ, D_MODEL), _F32)] + cast_shapes,
        compiler_params=pltpu.CompilerParams(
            dimension_semantics=("arbitrary", "arbitrary"),
            vmem_limit_bytes=V7X_VMEM_LIMIT_BYTES),
        name=f"mix_merge_l{layer}",
    )(x, n, y, ada, post_g, gate_w, gate_b, branch_w_out, w_o, *[job.src for job in cast_jobs])


def kernel(x, c, ada_w, ada_b, pre_g, post_g, ffn_w_in, ffn_w_out, mix_w_in, gate_w, gate_b, conv_w, conv_b, conv_ln_g, conv_ln_b, pool_group_w, pool_scale, sgu_ln_g, sgu_ln_b, sgu_w_s, sgu_b_s, sconv_w, branch_w_out, w_o):
    assert x.shape == (1, SEQ, D_MODEL) and c.shape == (1, D_MODEL)
    L, W = DEPTH, BRANCH_WIDTH
    h = x.reshape(SEQ, D_MODEL)
    ada = _ada_call(c.reshape(D_MODEL, 1), ada_w, ada_b.reshape(L, 1, ADA_WIDTH))

    sgu_b_s_t = jnp.swapaxes(sgu_b_s, 1, 2)
    gate_w_rows = gate_w.reshape(L, N_BRANCHES * D_MODEL, D_MODEL)
    branch_w_rows = branch_w_out.reshape(L, N_BRANCHES * W, D_MODEL)
    ffn_steps = (SEQ // FFN_TM) * (D_FF // FFN_TF)
    mixa_steps = SEQ // MIXA_TM
    mixb_steps = (SEQ // MIXB_TM) * (D_MODEL // MIXB_TN)

    w_in_b = ffn_w_in[0, 0].astype(_BF16)
    w_out_b = ffn_w_out[0, 0].astype(_BF16)
    for l in range(DEPTH):
        h, mix_in_b, gate_b16, branch_b16, w_o_b = _ffn_call(
            h, ada, pre_g, post_g, w_in_b, w_out_b, l, 0, 0, cast_jobs=(
                _cast_job(mix_w_in, (l,), ffn_steps), _cast_job(gate_w_rows, (l,), ffn_steps),
                _cast_job(branch_w_rows, (l,), ffn_steps), _cast_job(w_o, (l,), ffn_steps)))
        n, y, w_out_b = _mixa_call(
            h, ada, pre_g, mix_in_b, conv_w, conv_b, conv_ln_g, conv_ln_b,
            pool_group_w, pool_scale, sgu_ln_g, sgu_ln_b, sgu_w_s, sgu_b_s_t,
            sconv_w, l, cast_jobs=(_cast_job(ffn_w_out, (l, 1), mixa_steps),))
        h, w_in_b = _mixb_call(
            h, n, y, ada, post_g, gate_b16.reshape(N_BRANCHES, D_MODEL, D_MODEL), gate_b,
            branch_b16.reshape(N_BRANCHES, W, D_MODEL), w_o_b, l,
            cast_jobs=(_cast_job(ffn_w_in, (l, 1), mixb_steps),))
        next_jobs = () if l + 1 == DEPTH else (
            _cast_job(ffn_w_in, (l + 1, 0), ffn_steps), _cast_job(ffn_w_out, (l + 1, 0), ffn_steps))
        h, *next_w = _ffn_call(h, ada, pre_g, post_g, w_in_b, w_out_b, l, 1, 2, cast_jobs=next_jobs)
        if next_w:
            w_in_b, w_out_b = next_w
    return h.reshape(1, SEQ, D_MODEL)
```

```python
import functools
from typing import NamedTuple

import jax
import jax.numpy as jnp
from jax import lax
from jax.experimental import pallas as pl
from jax.experimental.pallas import tpu as pltpu

D_MODEL = 2048
SEQ = 16384
DEPTH = 2
BRANCH_WIDTH = 512
N_BRANCHES = 4
GROUP_WIDTH = 128
N_GROUPS = 4
CONV_KERNEL = 31
POOL_WINDOWS = (2, 4, 8, 16)
SGU_CHUNK = 128
SHORT_CONV_KERNEL = 3
D_FF = 5632
N_SUBLAYERS = 3
N_MOD = 3
ADA_WIDTH = N_SUBLAYERS * N_MOD * D_MODEL
MIX_IN_WIDTH = 8 * BRANCH_WIDTH
RMS_EPS = 1e-6
LN_EPS = 1e-5

V7X_VMEM_LIMIT_BYTES = 60 * 1024 * 1024
SUBLANES = 8
BF16_SUBLANES = 16

ADA_TN = 1024
FFN_TM = 1024
FFN_TF = 512
FFN_TN = 512
ROW_CHUNK = 16
FFN_FIRST_BLOCKS = (512, 512)
FFN_LAST_BLOCKS = (512, 512)
MIXB_LAST_BLOCKS = (256, 256)
CONV_ROWS = 64
MIXA_TM = 512
MIXB_TM = 512
MIXB_TN = 512
CONV_HALO = 32
POOL_HALO = 16
SCONV_HALO = 8

_BF16 = jnp.bfloat16
_F32 = jnp.float32


def _sigmoid(v):
    return 1.0 / (1.0 + jnp.exp(-v))


def _silu(v):
    return v * _sigmoid(v)


def _rms_norm(v, g):
    return v * lax.rsqrt(jnp.mean(v * v, axis=-1, keepdims=True) + RMS_EPS) * g


def _layer_norm(v, g, b):
    mu = jnp.mean(v, axis=-1, keepdims=True)
    vc = v - mu
    return vc * lax.rsqrt(jnp.mean(vc * vc, axis=-1, keepdims=True) + LN_EPS) * g + b


def _mod(ada_ref, sub, which):
    off = (sub * N_MOD + which) * D_MODEL
    return ada_ref[0, :, off:off + D_MODEL]


def _modulated_norm(x, pre_g_ref, ada_ref, sub):
    n = _rms_norm(x, pre_g_ref[0, sub:sub + 1, :])
    return n * (1.0 + _mod(ada_ref, sub, 1)) + _mod(ada_ref, sub, 0)


def _row_blocks(tm, sizes):
    assert sum(sizes) == tm
    starts = [sum(sizes[:k]) for k in range(len(sizes))]
    return [slice(s, s + n) for s, n in zip(starts, sizes)]


def _row_chunks(block):
    return [slice(r, r + ROW_CHUNK) for r in range(block.start, block.stop, ROW_CHUNK)]


def _residual_epilogue(x_ref, o_ref, gain, block):
    for rows in _row_chunks(block):
        y = o_ref[rows, :]
        inv = lax.rsqrt(jnp.mean(y * y, axis=-1, keepdims=True) + RMS_EPS)
        o_ref[rows, :] = x_ref[rows, :] + y * inv * gain


class _CastJob(NamedTuple):
    src: jax.Array
    prefix: tuple
    block_rows: int

    @property
    def rows(self):
        return self.src.shape[-2]

    @property
    def cols(self):
        return self.src.shape[-1]

    @property
    def n_blocks(self):
        return self.rows // self.block_rows


def _cast_job(src, prefix, n_steps):
    rows = src.shape[-2]
    block_rows = next(b for b in range(BF16_SUBLANES, rows + 1, BF16_SUBLANES)
                      if rows % b == 0 and rows // b <= n_steps)
    return _CastJob(src, tuple(prefix), block_rows)


def _cast_specs(jobs, grid):
    n_steps = 1
    for g in grid:
        n_steps *= g
    in_specs, out_specs, out_shapes = [], [], []
    for job in jobs:
        assert job.rows % job.block_rows == 0 and job.n_blocks <= n_steps
        last = job.n_blocks - 1

        def block_index(*ids, last=last):
            step = ids[0]
            for g, idx in zip(grid[1:], ids[1:]):
                step = step * g + idx
            return jnp.minimum(step, last)

        in_specs.append(pl.BlockSpec(
            (None,) * len(job.prefix) + (job.block_rows, job.cols),
            lambda *ids, job=job, block_index=block_index: job.prefix + (block_index(*ids), 0)))
        out_specs.append(pl.BlockSpec(
            (job.block_rows, job.cols), lambda *ids, block_index=block_index: (block_index(*ids), 0)))
        out_shapes.append(jax.ShapeDtypeStruct((job.rows, job.cols), _BF16))
    return in_specs, out_specs, out_shapes


def _run_casts(src_refs, dst_refs):
    for src, dst in zip(src_refs, dst_refs):
        dst[...] = src[...].astype(_BF16)


def _ada_kernel(c_ref, w_ref, b_ref, o_ref):
    cond = _silu(c_ref[...])
    o_ref[0] = jnp.sum(w_ref[0] * cond, axis=0, keepdims=True) + b_ref[0]


def _ada_call(c_col, ada_w, ada_b3):
    return pl.pallas_call(
        _ada_kernel,
        grid=(DEPTH, ADA_WIDTH // ADA_TN),
        in_specs=[
            pl.BlockSpec((D_MODEL, 1), lambda l, j: (0, 0)),
            pl.BlockSpec((1, D_MODEL, ADA_TN), lambda l, j: (l, 0, j)),
            pl.BlockSpec((1, 1, ADA_TN), lambda l, j: (l, 0, j)),
        ],
        out_specs=pl.BlockSpec((1, 1, ADA_TN), lambda l, j: (l, 0, j)),
        out_shape=jax.ShapeDtypeStruct((DEPTH, 1, ADA_WIDTH), _F32),
        compiler_params=pltpu.CompilerParams(
            dimension_semantics=("arbitrary", "arbitrary"),
            vmem_limit_bytes=V7X_VMEM_LIMIT_BYTES),
        name="ada",
    )(c_col, ada_w, ada_b3)


def _ffn_kernel(sub, n_cast, x_ref, ada_ref, pre_g_ref, post_g_ref, wg_ref, wu_ref, wo_ref, *rest):
    cast_src, (o_ref, *cast_dst), (n_scr,) = (
        rest[:n_cast], rest[n_cast:2 * n_cast + 1], rest[2 * n_cast + 1:])
    j = pl.program_id(1)
    last = pl.num_programs(1) - 1

    def matmuls(rows, first):
        n = n_scr[rows, :]
        hg = jnp.dot(n, wg_ref[...], preferred_element_type=_F32)
        hu = jnp.dot(n, wu_ref[...], preferred_element_type=_F32)
        a = (_silu(hg) * hu).astype(_BF16)
        for c in range(D_MODEL // FFN_TN):
            cols = slice(c * FFN_TN, (c + 1) * FFN_TN)
            part = jnp.dot(a, wo_ref[:, cols], preferred_element_type=_F32)
            if first:
                o_ref[rows, cols] = part
            else:
                o_ref[rows, cols] += part

    @pl.when(j == 0)
    def _():
        scale = pre_g_ref[0, sub:sub + 1, :] * (1.0 + _mod(ada_ref, sub, 1))
        shift = _mod(ada_ref, sub, 0)
        for block in _row_blocks(FFN_TM, FFN_FIRST_BLOCKS):
            for rows in _row_chunks(block):
                xv = x_ref[rows, :]
                inv = lax.rsqrt(jnp.mean(xv * xv, axis=-1, keepdims=True) + RMS_EPS)
                n_scr[rows, :] = (xv * inv * scale + shift).astype(_BF16)
            matmuls(block, first=True)
        _run_casts(cast_src, cast_dst)

    @pl.when(jnp.logical_and(j > 0, j < last))
    def _():
        matmuls(slice(0, FFN_TM), first=False)
        _run_casts(cast_src, cast_dst)

    @pl.when(j == last)
    def _():
        gain = post_g_ref[0, sub:sub + 1, :] * (0.5 * _mod(ada_ref, sub, 2))
        for block in _row_blocks(FFN_TM, FFN_LAST_BLOCKS):
            matmuls(block, first=False)
            _residual_epilogue(x_ref, o_ref, gain, block)
        _run_casts(cast_src, cast_dst)


def _ffn_call(x, ada, pre_g, post_g, w_in, w_out, layer, which, sub, cast_jobs=()):
    n_f = D_FF // FFN_TF
    grid = (SEQ // FFN_TM, n_f)
    cast_in, cast_out, cast_shapes = _cast_specs(cast_jobs, grid)
    return pl.pallas_call(
        functools.partial(_ffn_kernel, sub, len(cast_jobs)),
        grid=grid,
        in_specs=[
            pl.BlockSpec((FFN_TM, D_MODEL), lambda i, j: (i, 0)),
            pl.BlockSpec((1, 1, ADA_WIDTH), lambda i, j: (layer, 0, 0)),
            pl.BlockSpec((1, N_SUBLAYERS, D_MODEL), lambda i, j: (layer, 0, 0)),
            pl.BlockSpec((1, N_SUBLAYERS, D_MODEL), lambda i, j: (layer, 0, 0)),
            pl.BlockSpec((D_MODEL, FFN_TF), lambda i, j: (0, j)),
            pl.BlockSpec((D_MODEL, FFN_TF), lambda i, j: (0, j + n_f)),
            pl.BlockSpec((FFN_TF, D_MODEL), lambda i, j: (j, 0)),
        ] + cast_in,
        out_specs=[pl.BlockSpec((FFN_TM, D_MODEL), lambda i, j: (i, 0))] + cast_out,
        out_shape=[jax.ShapeDtypeStruct((SEQ, D_MODEL), _F32)] + cast_shapes,
        scratch_shapes=[pltpu.VMEM((FFN_TM, D_MODEL), _BF16)],
        compiler_params=pltpu.CompilerParams(
            dimension_semantics=("arbitrary", "arbitrary"),
            vmem_limit_bytes=V7X_VMEM_LIMIT_BYTES),
        name=f"ffn_l{layer}_{which}",
    )(x, ada, pre_g, post_g, w_in, w_in, w_out, *[job.src for job in cast_jobs])


def _shift_history(buf, halo, tm):
    buf[0:halo, :] = buf[tm:tm + halo, :]


def _mixa_kernel(layer, n_cast, x_ref, ada_ref, pre_g_ref, w_ref, conv_w_ref, conv_b_ref, cln_g_ref,
                 cln_b_ref, pool_w_ref, pool_s_ref, sln_g_ref, sln_b_ref, ws_ref, bs_ref,
                 sconv_w_ref, *rest):
    cast_src, (n_ref, y_ref, *cast_dst), (conv_buf, pool_buf, sconv_buf) = (
        rest[:n_cast], rest[n_cast:2 * n_cast + 2], rest[2 * n_cast + 2:])
    tm = MIXA_TM
    W = BRANCH_WIDTH
    lrow = slice(layer, layer + 1)
    i = pl.program_id(0)

    @pl.when(i == 0)
    def _():
        conv_buf[0:CONV_HALO, :] = jnp.zeros((CONV_HALO, W), _F32)
        pool_buf[0:POOL_HALO, :] = jnp.zeros((POOL_HALO, W), _F32)
        sconv_buf[0:SCONV_HALO, :] = jnp.zeros((SCONV_HALO, W), _F32)

    n = _modulated_norm(x_ref[...], pre_g_ref, ada_ref, 1).astype(_BF16)
    n_ref[...] = n

    p = jnp.dot(n, w_ref[:, 0:2 * W], preferred_element_type=_F32)
    conv_buf[CONV_HALO:CONV_HALO + tm, :] = p[:, 0:W] * _sigmoid(p[:, W:2 * W])
    base = CONV_HALO - (CONV_KERNEL - 1)
    for r0 in range(0, tm, CONV_ROWS):
        acc = None
        for r in range(SUBLANES):
            n_rows = CONV_ROWS if r == 0 else CONV_ROWS + SUBLANES
            part = None
            for k in range(CONV_KERNEL):
                if (base + k) % SUBLANES != r:
                    continue
                start = r0 + (base + k) - r
                term = conv_w_ref[0, k:k + 1, :] * conv_buf[start:start + n_rows, :]
                part = term if part is None else part + term
            if r != 0:
                part = part[r:r + CONV_ROWS, :]
            acc = part if acc is None else acc + part
        acc = acc + conv_b_ref[lrow, :]
        y_ref[r0:r0 + CONV_ROWS, 0:W] = _silu(
            _layer_norm(acc, cln_g_ref[lrow, :], cln_b_ref[lrow, :])).astype(_BF16)
    _shift_history(conv_buf, CONV_HALO, tm)

    p_pool = jnp.dot(n, w_ref[:, 2 * W:3 * W], preferred_element_type=_F32)
    p_sgu = jnp.dot(n, w_ref[:, 3 * W:5 * W], preferred_element_type=_F32)
    p_sconv = jnp.dot(n, w_ref[:, 5 * W:8 * W], preferred_element_type=_F32)
    p = p_pool
    pool_buf[POOL_HALO:POOL_HALO + tm, :] = p
    pos1 = (lax.broadcasted_iota(jnp.int32, (tm, 1), 0) + (i * tm + 1)).astype(_F32)
    for g, win in enumerate(POOL_WINDOWS):
        lanes = slice(g * GROUP_WIDTH, (g + 1) * GROUP_WIDTH)
        tok = p[:, lanes]
        tot = tok
        for d in range(1, win):
            tot = tot + pool_buf[POOL_HALO - d:POOL_HALO - d + tm, lanes]
        pooled = tot / jnp.minimum(pos1, float(win)) - tok
        mixed = jnp.dot(pooled.astype(_BF16), pool_w_ref[0, g].astype(_BF16),
                        preferred_element_type=_F32)
        off = W + g * GROUP_WIDTH
        y_ref[:, off:off + GROUP_WIDTH] = (mixed * pool_s_ref[lrow, lanes]).astype(_BF16)
    _shift_history(pool_buf, POOL_HALO, tm)

    p = p_sgu
    p = 0.5 * p * (1.0 + lax.erf(p * (2.0 ** -0.5)))
    u = p[:, 0:W]
    v = _layer_norm(p[:, W:2 * W], sln_g_ref[lrow, :], sln_b_ref[lrow, :]).astype(_BF16)
    n_chunks = tm // SGU_CHUNK
    row = lax.broadcasted_iota(jnp.int32, (SGU_CHUNK, SGU_CHUNK), 0)
    col = lax.broadcasted_iota(jnp.int32, (SGU_CHUNK, SGU_CHUNK), 1)
    for g in range(N_GROUPS):
        lanes = slice(g * GROUP_WIDTH, (g + 1) * GROUP_WIDTH)
        w_causal = jnp.where(col <= row, ws_ref[0, g], 0.0).astype(_BF16)
        vg = jnp.concatenate(
            [v[c * SGU_CHUNK:(c + 1) * SGU_CHUNK, lanes] for c in range(n_chunks)], axis=1)
        sv = jnp.dot(w_causal, vg, preferred_element_type=_F32) + bs_ref[0, :, g:g + 1]
        for c in range(n_chunks):
            rows = slice(c * SGU_CHUNK, (c + 1) * SGU_CHUNK)
            y_ref[rows, 2 * W + g * GROUP_WIDTH:2 * W + (g + 1) * GROUP_WIDTH] = (
                u[rows, lanes] * sv[:, c * GROUP_WIDTH:(c + 1) * GROUP_WIDTH]).astype(_BF16)

    p = p_sconv
    z = p[:, W:2 * W] * p[:, 2 * W:3 * W]
    sconv_buf[SCONV_HALO:SCONV_HALO + tm, :] = z
    base = SCONV_HALO - (SHORT_CONV_KERNEL - 1)
    acc = sconv_w_ref[0, SHORT_CONV_KERNEL - 1:SHORT_CONV_KERNEL, :] * z
    for k in range(SHORT_CONV_KERNEL - 1):
        acc = acc + sconv_w_ref[0, k:k + 1, :] * sconv_buf[base + k:base + k + tm, :]
    y_ref[:, 3 * W:4 * W] = (p[:, 0:W] * acc).astype(_BF16)
    _shift_history(sconv_buf, SCONV_HALO, tm)
    _run_casts(cast_src, cast_dst)


def _mixa_call(x, ada, pre_g, mix_w_in, conv_w, conv_b, conv_ln_g, conv_ln_b, pool_group_w,
               pool_scale, sgu_ln_g, sgu_ln_b, sgu_w_s, sgu_b_s_t, sconv_w, layer, cast_jobs=()):
    tm, W = MIXA_TM, BRANCH_WIDTH
    grid = (SEQ // tm,)
    cast_in, cast_out, cast_shapes = _cast_specs(cast_jobs, grid)
    row_vec = pl.BlockSpec((DEPTH, W), lambda i: (0, 0))
    return pl.pallas_call(
        functools.partial(_mixa_kernel, layer, len(cast_jobs)),
        grid=grid,
        in_specs=[
            pl.BlockSpec((tm, D_MODEL), lambda i: (i, 0)),
            pl.BlockSpec((1, 1, ADA_WIDTH), lambda i: (layer, 0, 0)),
            pl.BlockSpec((1, N_SUBLAYERS, D_MODEL), lambda i: (layer, 0, 0)),
            pl.BlockSpec((D_MODEL, MIX_IN_WIDTH), lambda i: (0, 0), pipeline_mode=pl.Buffered(1)),
            pl.BlockSpec((1, CONV_KERNEL, W), lambda i: (layer, 0, 0)),
            row_vec, row_vec, row_vec,
            pl.BlockSpec((1, N_GROUPS, GROUP_WIDTH, GROUP_WIDTH), lambda i: (layer, 0, 0, 0)),
            row_vec, row_vec, row_vec,
            pl.BlockSpec((1, N_GROUPS, SGU_CHUNK, SGU_CHUNK), lambda i: (layer, 0, 0, 0)),
            pl.BlockSpec((1, SGU_CHUNK, N_GROUPS), lambda i: (layer, 0, 0)),
            pl.BlockSpec((1, SHORT_CONV_KERNEL, W), lambda i: (layer, 0, 0)),
        ] + cast_in,
        out_specs=[
            pl.BlockSpec((tm, D_MODEL), lambda i: (i, 0)),
            pl.BlockSpec((tm, N_BRANCHES * W), lambda i: (i, 0)),
        ] + cast_out,
        out_shape=[
            jax.ShapeDtypeStruct((SEQ, D_MODEL), _BF16),
            jax.ShapeDtypeStruct((SEQ, N_BRANCHES * W), _BF16),
        ] + cast_shapes,
        scratch_shapes=[
            pltpu.VMEM((CONV_HALO + tm, W), _F32),
            pltpu.VMEM((POOL_HALO + tm, W), _F32),
            pltpu.VMEM((SCONV_HALO + tm, W), _F32),
        ],
        compiler_params=pltpu.CompilerParams(
            dimension_semantics=("arbitrary",),
            vmem_limit_bytes=V7X_VMEM_LIMIT_BYTES),
        name=f"mix_branches_l{layer}",
    )(x, ada, pre_g, mix_w_in, conv_w, conv_b, conv_ln_g, conv_ln_b, pool_group_w, pool_scale,
      sgu_ln_g, sgu_ln_b, sgu_w_s, sgu_b_s_t, sconv_w, *[job.src for job in cast_jobs])


def _mixb_kernel(n_cast, x_ref, n_ref, y_ref, ada_ref, post_g_ref, gw_ref, gb_ref, bw_ref, wo_ref,
                 *rest):
    cast_src, (o_ref, *cast_dst) = rest[:n_cast], rest[n_cast:]
    W = BRANCH_WIDTH
    j = pl.program_id(1)
    last = pl.num_programs(1) - 1

    def matmuls(rows, first):
        n = n_ref[rows, :]
        merged = None
        for b in range(N_BRANCHES):
            gate = _sigmoid(
                jnp.dot(n, gw_ref[b], preferred_element_type=_F32) + gb_ref[0, b:b + 1, :])
            term = gate * jnp.dot(y_ref[rows, b * W:(b + 1) * W], bw_ref[b],
                                  preferred_element_type=_F32)
            merged = term if merged is None else merged + term
        merged = merged.astype(_BF16)
        for c in range(D_MODEL // FFN_TN):
            cols = slice(c * FFN_TN, (c + 1) * FFN_TN)
            part = jnp.dot(merged, wo_ref[:, cols], preferred_element_type=_F32)
            if first:
                o_ref[rows, cols] = part
            else:
                o_ref[rows, cols] += part

    @pl.when(j == 0)
    def _():
        matmuls(slice(0, MIXB_TM), first=True)
        _run_casts(cast_src, cast_dst)

    @pl.when(jnp.logical_and(j > 0, j < last))
    def _():
        matmuls(slice(0, MIXB_TM), first=False)
        _run_casts(cast_src, cast_dst)

    @pl.when(j == last)
    def _():
        gain = post_g_ref[0, 1:2, :] * _mod(ada_ref, 1, 2)
        for block in _row_blocks(MIXB_TM, MIXB_LAST_BLOCKS):
            matmuls(block, first=False)
            _residual_epilogue(x_ref, o_ref, gain, block)
        _run_casts(cast_src, cast_dst)


def _mixb_call(x, n, y, ada, post_g, gate_w, gate_b, branch_w_out, w_o, layer, cast_jobs=()):
    tm, tn, W = MIXB_TM, MIXB_TN, BRANCH_WIDTH
    grid = (SEQ // tm, D_MODEL // tn)
    cast_in, cast_out, cast_shapes = _cast_specs(cast_jobs, grid)
    return pl.pallas_call(
        functools.partial(_mixb_kernel, len(cast_jobs)),
        grid=grid,
        in_specs=[
            pl.BlockSpec((tm, D_MODEL), lambda i, j: (i, 0)),
            pl.BlockSpec((tm, D_MODEL), lambda i, j: (i, 0)),
            pl.BlockSpec((tm, N_BRANCHES * W), lambda i, j: (i, 0)),
            pl.BlockSpec((1, 1, ADA_WIDTH), lambda i, j: (layer, 0, 0)),
            pl.BlockSpec((1, N_SUBLAYERS, D_MODEL), lambda i, j: (layer, 0, 0)),
            pl.BlockSpec((N_BRANCHES, D_MODEL, tn), lambda i, j: (0, 0, j)),
            pl.BlockSpec((1, N_BRANCHES, tn), lambda i, j: (layer, 0, j)),
            pl.BlockSpec((N_BRANCHES, W, tn), lambda i, j: (0, 0, j)),
            pl.BlockSpec((tn, D_MODEL), lambda i, j: (j, 0)),
        ] + cast_in,
        out_specs=[pl.BlockSpec((tm, D_MODEL), lambda i, j: (i, 0))] + cast_out,
        out_shape=[jax.ShapeDtypeStruct((SEQ, D_MODEL), _F32)] + cast_shapes,
        compiler_params=pltpu.CompilerParams(
            dimension_semantics=("arbitrary", "arbitrary"),
            vmem_limit_bytes=V7X_VMEM_LIMIT_BYTES),
        name=f"mix_merge_l{layer}",
    )(x, n, y, ada, post_g, gate_w, gate_b, branch_w_out, w_o, *[job.src for job in cast_jobs])


def kernel(x, c, ada_w, ada_b, pre_g, post_g, ffn_w_in, ffn_w_out, mix_w_in, gate_w, gate_b, conv_w, conv_b, conv_ln_g, conv_ln_b, pool_group_w, pool_scale, sgu_ln_g, sgu_ln_b, sgu_w_s, sgu_b_s, sconv_w, branch_w_out, w_o):
    assert x.shape == (1, SEQ, D_MODEL) and c.shape == (1, D_MODEL)
    L, W = DEPTH, BRANCH_WIDTH
    h = x.reshape(SEQ, D_MODEL)
    ada = _ada_call(c.reshape(D_MODEL, 1), ada_w, ada_b.reshape(L, 1, ADA_WIDTH))

    sgu_b_s_t = jnp.swapaxes(sgu_b_s, 1, 2)
    gate_w_rows = gate_w.reshape(L, N_BRANCHES * D_MODEL, D_MODEL)
    branch_w_rows = branch_w_out.reshape(L, N_BRANCHES * W, D_MODEL)
    ffn_steps = (SEQ // FFN_TM) * (D_FF // FFN_TF)
    mixa_steps = SEQ // MIXA_TM
    mixb_steps = (SEQ // MIXB_TM) * (D_MODEL // MIXB_TN)

    w_in_b = ffn_w_in[0, 0].astype(_BF16)
    w_out_b = ffn_w_out[0, 0].astype(_BF16)
    for l in range(DEPTH):
        h, mix_in_b, gate_b16 = _ffn_call(
            h, ada, pre_g, post_g, w_in_b, w_out_b, l, 0, 0, cast_jobs=(
                _cast_job(mix_w_in, (l,), ffn_steps), _cast_job(gate_w_rows, (l,), ffn_steps)))
        n, y, w_out_b, branch_b16, w_o_b = _mixa_call(
            h, ada, pre_g, mix_in_b, conv_w, conv_b, conv_ln_g, conv_ln_b,
            pool_group_w, pool_scale, sgu_ln_g, sgu_ln_b, sgu_w_s, sgu_b_s_t,
            sconv_w, l, cast_jobs=(
                _cast_job(ffn_w_out, (l, 1), mixa_steps), _cast_job(branch_w_rows, (l,), mixa_steps),
                _cast_job(w_o, (l,), mixa_steps)))
        h, w_in_b = _mixb_call(
            h, n, y, ada, post_g, gate_b16.reshape(N_BRANCHES, D_MODEL, D_MODEL), gate_b,
            branch_b16.reshape(N_BRANCHES, W, D_MODEL), w_o_b, l,
            cast_jobs=(_cast_job(ffn_w_in, (l, 1), mixb_steps),))
        next_jobs = () if l + 1 == DEPTH else (
            _cast_job(ffn_w_in, (l + 1, 0), ffn_steps), _cast_job(ffn_w_out, (l + 1, 0), ffn_steps))
        h, *next_w = _ffn_call(h, ada, pre_g, post_g, w_in_b, w_out_b, l, 1, 2, cast_jobs=next_jobs)
        if next_w:
            w_in_b, w_out_b = next_w
    return h.reshape(1, SEQ, D_MODEL)
```

```python
import functools
from typing import NamedTuple

import jax
import jax.numpy as jnp
from jax import lax
from jax.experimental import pallas as pl
from jax.experimental.pallas import tpu as pltpu

D_MODEL = 2048
SEQ = 16384
DEPTH = 2
BRANCH_WIDTH = 512
N_BRANCHES = 4
GROUP_WIDTH = 128
N_GROUPS = 4
CONV_KERNEL = 31
POOL_WINDOWS = (2, 4, 8, 16)
SGU_CHUNK = 128
SHORT_CONV_KERNEL = 3
D_FF = 5632
N_SUBLAYERS = 3
N_MOD = 3
ADA_WIDTH = N_SUBLAYERS * N_MOD * D_MODEL
MIX_IN_WIDTH = 8 * BRANCH_WIDTH
RMS_EPS = 1e-6
LN_EPS = 1e-5

V7X_VMEM_LIMIT_BYTES = 60 * 1024 * 1024
SUBLANES = 8
BF16_SUBLANES = 16

ADA_TN = 2048
FFN_TM = 1024
FFN_TF = 512
FFN_TN = 512
ROW_CHUNK = 16
FFN_FIRST_BLOCKS = (512, 512)
FFN_LAST_BLOCKS = (512, 512)
MIXB_LAST_BLOCKS = (256, 256)
CONV_ROWS = 64
MIXA_TM = 512
MIXB_TM = 512
MIXB_TN = 512
CONV_HALO = 32
POOL_HALO = 16
SCONV_HALO = 8

_BF16 = jnp.bfloat16
_F32 = jnp.float32


def _sigmoid(v):
    return 1.0 / (1.0 + jnp.exp(-v))


def _silu(v):
    return v * _sigmoid(v)


def _rms_norm(v, g):
    return v * lax.rsqrt(jnp.mean(v * v, axis=-1, keepdims=True) + RMS_EPS) * g


def _layer_norm(v, g, b):
    mu = jnp.mean(v, axis=-1, keepdims=True)
    vc = v - mu
    return vc * lax.rsqrt(jnp.mean(vc * vc, axis=-1, keepdims=True) + LN_EPS) * g + b


def _mod(ada_ref, sub, which):
    off = (sub * N_MOD + which) * D_MODEL
    return ada_ref[0, :, off:off + D_MODEL]


def _modulated_norm(x, pre_g_ref, ada_ref, sub):
    n = _rms_norm(x, pre_g_ref[0, sub:sub + 1, :])
    return n * (1.0 + _mod(ada_ref, sub, 1)) + _mod(ada_ref, sub, 0)


def _row_blocks(tm, sizes):
    assert sum(sizes) == tm
    starts = [sum(sizes[:k]) for k in range(len(sizes))]
    return [slice(s, s + n) for s, n in zip(starts, sizes)]


def _row_chunks(block):
    return [slice(r, r + ROW_CHUNK) for r in range(block.start, block.stop, ROW_CHUNK)]


def _residual_epilogue(x_ref, o_ref, gain, block):
    for rows in _row_chunks(block):
        y = o_ref[rows, :]
        inv = lax.rsqrt(jnp.mean(y * y, axis=-1, keepdims=True) + RMS_EPS)
        o_ref[rows, :] = x_ref[rows, :] + y * inv * gain


class _CastJob(NamedTuple):
    src: jax.Array
    prefix: tuple
    block_rows: int

    @property
    def rows(self):
        return self.src.shape[-2]

    @property
    def cols(self):
        return self.src.shape[-1]

    @property
    def n_blocks(self):
        return self.rows // self.block_rows


def _cast_job(src, prefix, n_steps):
    rows = src.shape[-2]
    block_rows = next(b for b in range(BF16_SUBLANES, rows + 1, BF16_SUBLANES)
                      if rows % b == 0 and rows // b <= n_steps)
    return _CastJob(src, tuple(prefix), block_rows)


def _cast_specs(jobs, grid):
    n_steps = 1
    for g in grid:
        n_steps *= g
    in_specs, out_specs, out_shapes = [], [], []
    for job in jobs:
        assert job.rows % job.block_rows == 0 and job.n_blocks <= n_steps
        last = job.n_blocks - 1

        def block_index(*ids, last=last):
            step = ids[0]
            for g, idx in zip(grid[1:], ids[1:]):
                step = step * g + idx
            return jnp.minimum(step, last)

        in_specs.append(pl.BlockSpec(
            (None,) * len(job.prefix) + (job.block_rows, job.cols),
            lambda *ids, job=job, block_index=block_index: job.prefix + (block_index(*ids), 0)))
        out_specs.append(pl.BlockSpec(
            (job.block_rows, job.cols), lambda *ids, block_index=block_index: (block_index(*ids), 0)))
        out_shapes.append(jax.ShapeDtypeStruct((job.rows, job.cols), _BF16))
    return in_specs, out_specs, out_shapes


def _run_casts(src_refs, dst_refs):
    for src, dst in zip(src_refs, dst_refs):
        dst[...] = src[...].astype(_BF16)


def _ada_kernel(c_ref, w_ref, b_ref, o_ref):
    cond = _silu(c_ref[...])
    o_ref[0] = jnp.sum(w_ref[0] * cond, axis=0, keepdims=True) + b_ref[0]


def _ada_call(c_col, ada_w, ada_b3):
    return pl.pallas_call(
        _ada_kernel,
        grid=(DEPTH, ADA_WIDTH // ADA_TN),
        in_specs=[
            pl.BlockSpec((D_MODEL, 1), lambda l, j: (0, 0)),
            pl.BlockSpec((1, D_MODEL, ADA_TN), lambda l, j: (l, 0, j)),
            pl.BlockSpec((1, 1, ADA_TN), lambda l, j: (l, 0, j)),
        ],
        out_specs=pl.BlockSpec((1, 1, ADA_TN), lambda l, j: (l, 0, j)),
        out_shape=jax.ShapeDtypeStruct((DEPTH, 1, ADA_WIDTH), _F32),
        compiler_params=pltpu.CompilerParams(
            dimension_semantics=("arbitrary", "arbitrary"),
            vmem_limit_bytes=V7X_VMEM_LIMIT_BYTES),
        name="ada",
    )(c_col, ada_w, ada_b3)


def _ffn_kernel(sub, n_cast, x_ref, ada_ref, pre_g_ref, post_g_ref, wg_ref, wu_ref, wo_ref, *rest):
    cast_src, (o_ref, *cast_dst), (n_scr,) = (
        rest[:n_cast], rest[n_cast:2 * n_cast + 1], rest[2 * n_cast + 1:])
    j = pl.program_id(1)
    last = pl.num_programs(1) - 1

    def matmuls(rows, first):
        n = n_scr[rows, :]
        hg = jnp.dot(n, wg_ref[...], preferred_element_type=_F32)
        hu = jnp.dot(n, wu_ref[...], preferred_element_type=_F32)
        a = (_silu(hg) * hu).astype(_BF16)
        for c in range(D_MODEL // FFN_TN):
            cols = slice(c * FFN_TN, (c + 1) * FFN_TN)
            part = jnp.dot(a, wo_ref[:, cols], preferred_element_type=_F32)
            if first:
                o_ref[rows, cols] = part
            else:
                o_ref[rows, cols] += part

    @pl.when(j == 0)
    def _():
        scale = pre_g_ref[0, sub:sub + 1, :] * (1.0 + _mod(ada_ref, sub, 1))
        shift = _mod(ada_ref, sub, 0)
        for block in _row_blocks(FFN_TM, FFN_FIRST_BLOCKS):
            for rows in _row_chunks(block):
                xv = x_ref[rows, :]
                inv = lax.rsqrt(jnp.mean(xv * xv, axis=-1, keepdims=True) + RMS_EPS)
                n_scr[rows, :] = (xv * inv * scale + shift).astype(_BF16)
            matmuls(block, first=True)
        _run_casts(cast_src, cast_dst)

    @pl.when(jnp.logical_and(j > 0, j < last))
    def _():
        matmuls(slice(0, FFN_TM), first=False)
        _run_casts(cast_src, cast_dst)

    @pl.when(j == last)
    def _():
        gain = post_g_ref[0, sub:sub + 1, :] * (0.5 * _mod(ada_ref, sub, 2))
        for block in _row_blocks(FFN_TM, FFN_LAST_BLOCKS):
            matmuls(block, first=False)
            _residual_epilogue(x_ref, o_ref, gain, block)
        _run_casts(cast_src, cast_dst)


def _ffn_call(x, ada, pre_g, post_g, w_in, w_out, layer, which, sub, cast_jobs=()):
    n_f = D_FF // FFN_TF
    grid = (SEQ // FFN_TM, n_f)
    cast_in, cast_out, cast_shapes = _cast_specs(cast_jobs, grid)
    return pl.pallas_call(
        functools.partial(_ffn_kernel, sub, len(cast_jobs)),
        grid=grid,
        in_specs=[
            pl.BlockSpec((FFN_TM, D_MODEL), lambda i, j: (i, 0)),
            pl.BlockSpec((1, 1, ADA_WIDTH), lambda i, j: (layer, 0, 0)),
            pl.BlockSpec((1, N_SUBLAYERS, D_MODEL), lambda i, j: (layer, 0, 0)),
            pl.BlockSpec((1, N_SUBLAYERS, D_MODEL), lambda i, j: (layer, 0, 0)),
            pl.BlockSpec((D_MODEL, FFN_TF), lambda i, j: (0, j)),
            pl.BlockSpec((D_MODEL, FFN_TF), lambda i, j: (0, j + n_f)),
            pl.BlockSpec((FFN_TF, D_MODEL), lambda i, j: (j, 0)),
        ] + cast_in,
        out_specs=[pl.BlockSpec((FFN_TM, D_MODEL), lambda i, j: (i, 0))] + cast_out,
        out_shape=[jax.ShapeDtypeStruct((SEQ, D_MODEL), _F32)] + cast_shapes,
        scratch_shapes=[pltpu.VMEM((FFN_TM, D_MODEL), _BF16)],
        compiler_params=pltpu.CompilerParams(
            dimension_semantics=("arbitrary", "arbitrary"),
            vmem_limit_bytes=V7X_VMEM_LIMIT_BYTES),
        name=f"ffn_l{layer}_{which}",
    )(x, ada, pre_g, post_g, w_in, w_in, w_out, *[job.src for job in cast_jobs])


def _shift_history(buf, halo, tm):
    buf[0:halo, :] = buf[tm:tm + halo, :]


def _mixa_kernel(layer, n_cast, x_ref, ada_ref, pre_g_ref, w_ref, conv_w_ref, conv_b_ref, cln_g_ref,
                 cln_b_ref, pool_w_ref, pool_s_ref, sln_g_ref, sln_b_ref, ws_ref, bs_ref,
                 sconv_w_ref, *rest):
    cast_src, (n_ref, y_ref, *cast_dst), (conv_buf, pool_buf, sconv_buf) = (
        rest[:n_cast], rest[n_cast:2 * n_cast + 2], rest[2 * n_cast + 2:])
    tm = MIXA_TM
    W = BRANCH_WIDTH
    lrow = slice(layer, layer + 1)
    i = pl.program_id(0)

    @pl.when(i == 0)
    def _():
        conv_buf[0:CONV_HALO, :] = jnp.zeros((CONV_HALO, W), _F32)
        pool_buf[0:POOL_HALO, :] = jnp.zeros((POOL_HALO, W), _F32)
        sconv_buf[0:SCONV_HALO, :] = jnp.zeros((SCONV_HALO, W), _F32)

    n = _modulated_norm(x_ref[...], pre_g_ref, ada_ref, 1).astype(_BF16)
    n_ref[...] = n

    p = jnp.dot(n, w_ref[:, 0:2 * W], preferred_element_type=_F32)
    conv_buf[CONV_HALO:CONV_HALO + tm, :] = p[:, 0:W] * _sigmoid(p[:, W:2 * W])
    base = CONV_HALO - (CONV_KERNEL - 1)
    for r0 in range(0, tm, CONV_ROWS):
        acc = None
        for r in range(SUBLANES):
            n_rows = CONV_ROWS if r == 0 else CONV_ROWS + SUBLANES
            part = None
            for k in range(CONV_KERNEL):
                if (base + k) % SUBLANES != r:
                    continue
                start = r0 + (base + k) - r
                term = conv_w_ref[0, k:k + 1, :] * conv_buf[start:start + n_rows, :]
                part = term if part is None else part + term
            if r != 0:
                part = part[r:r + CONV_ROWS, :]
            acc = part if acc is None else acc + part
        acc = acc + conv_b_ref[lrow, :]
        y_ref[r0:r0 + CONV_ROWS, 0:W] = _silu(
            _layer_norm(acc, cln_g_ref[lrow, :], cln_b_ref[lrow, :])).astype(_BF16)
    _shift_history(conv_buf, CONV_HALO, tm)

    p_pool = jnp.dot(n, w_ref[:, 2 * W:3 * W], preferred_element_type=_F32)
    p_sgu = jnp.dot(n, w_ref[:, 3 * W:5 * W], preferred_element_type=_F32)
    p_sconv = jnp.dot(n, w_ref[:, 5 * W:8 * W], preferred_element_type=_F32)
    p = p_pool
    pool_buf[POOL_HALO:POOL_HALO + tm, :] = p
    pos1 = (lax.broadcasted_iota(jnp.int32, (tm, 1), 0) + (i * tm + 1)).astype(_F32)
    for g, win in enumerate(POOL_WINDOWS):
        lanes = slice(g * GROUP_WIDTH, (g + 1) * GROUP_WIDTH)
        tok = p[:, lanes]
        tot = tok
        for d in range(1, win):
            tot = tot + pool_buf[POOL_HALO - d:POOL_HALO - d + tm, lanes]
        pooled = tot / jnp.minimum(pos1, float(win)) - tok
        mixed = jnp.dot(pooled.astype(_BF16), pool_w_ref[0, g].astype(_BF16),
                        preferred_element_type=_F32)
        off = W + g * GROUP_WIDTH
        y_ref[:, off:off + GROUP_WIDTH] = (mixed * pool_s_ref[lrow, lanes]).astype(_BF16)
    _shift_history(pool_buf, POOL_HALO, tm)

    p = p_sgu
    p = 0.5 * p * (1.0 + lax.erf(p * (2.0 ** -0.5)))
    u = p[:, 0:W]
    v = _layer_norm(p[:, W:2 * W], sln_g_ref[lrow, :], sln_b_ref[lrow, :]).astype(_BF16)
    n_chunks = tm // SGU_CHUNK
    row = lax.broadcasted_iota(jnp.int32, (SGU_CHUNK, SGU_CHUNK), 0)
    col = lax.broadcasted_iota(jnp.int32, (SGU_CHUNK, SGU_CHUNK), 1)
    for g in range(N_GROUPS):
        lanes = slice(g * GROUP_WIDTH, (g + 1) * GROUP_WIDTH)
        w_causal = jnp.where(col <= row, ws_ref[0, g], 0.0).astype(_BF16)
        vg = jnp.concatenate(
            [v[c * SGU_CHUNK:(c + 1) * SGU_CHUNK, lanes] for c in range(n_chunks)], axis=1)
        sv = jnp.dot(w_causal, vg, preferred_element_type=_F32) + bs_ref[0, :, g:g + 1]
        for c in range(n_chunks):
            rows = slice(c * SGU_CHUNK, (c + 1) * SGU_CHUNK)
            y_ref[rows, 2 * W + g * GROUP_WIDTH:2 * W + (g + 1) * GROUP_WIDTH] = (
                u[rows, lanes] * sv[:, c * GROUP_WIDTH:(c + 1) * GROUP_WIDTH]).astype(_BF16)

    p = p_sconv
    z = p[:, W:2 * W] * p[:, 2 * W:3 * W]
    sconv_buf[SCONV_HALO:SCONV_HALO + tm, :] = z
    base = SCONV_HALO - (SHORT_CONV_KERNEL - 1)
    acc = sconv_w_ref[0, SHORT_CONV_KERNEL - 1:SHORT_CONV_KERNEL, :] * z
    for k in range(SHORT_CONV_KERNEL - 1):
        acc = acc + sconv_w_ref[0, k:k + 1, :] * sconv_buf[base + k:base + k + tm, :]
    y_ref[:, 3 * W:4 * W] = (p[:, 0:W] * acc).astype(_BF16)
    _shift_history(sconv_buf, SCONV_HALO, tm)
    _run_casts(cast_src, cast_dst)


def _mixa_call(x, ada, pre_g, mix_w_in, conv_w, conv_b, conv_ln_g, conv_ln_b, pool_group_w,
               pool_scale, sgu_ln_g, sgu_ln_b, sgu_w_s, sgu_b_s_t, sconv_w, layer, cast_jobs=()):
    tm, W = MIXA_TM, BRANCH_WIDTH
    grid = (SEQ // tm,)
    cast_in, cast_out, cast_shapes = _cast_specs(cast_jobs, grid)
    row_vec = pl.BlockSpec((DEPTH, W), lambda i: (0, 0))
    return pl.pallas_call(
        functools.partial(_mixa_kernel, layer, len(cast_jobs)),
        grid=grid,
        in_specs=[
            pl.BlockSpec((tm, D_MODEL), lambda i: (i, 0)),
            pl.BlockSpec((1, 1, ADA_WIDTH), lambda i: (layer, 0, 0)),
            pl.BlockSpec((1, N_SUBLAYERS, D_MODEL), lambda i: (layer, 0, 0)),
            pl.BlockSpec((D_MODEL, MIX_IN_WIDTH), lambda i: (0, 0), pipeline_mode=pl.Buffered(1)),
            pl.BlockSpec((1, CONV_KERNEL, W), lambda i: (layer, 0, 0)),
            row_vec, row_vec, row_vec,
            pl.BlockSpec((1, N_GROUPS, GROUP_WIDTH, GROUP_WIDTH), lambda i: (layer, 0, 0, 0)),
            row_vec, row_vec, row_vec,
            pl.BlockSpec((1, N_GROUPS, SGU_CHUNK, SGU_CHUNK), lambda i: (layer, 0, 0, 0)),
            pl.BlockSpec((1, SGU_CHUNK, N_GROUPS), lambda i: (layer, 0, 0)),
            pl.BlockSpec((1, SHORT_CONV_KERNEL, W), lambda i: (layer, 0, 0)),
        ] + cast_in,
        out_specs=[
            pl.BlockSpec((tm, D_MODEL), lambda i: (i, 0)),
            pl.BlockSpec((tm, N_BRANCHES * W), lambda i: (i, 0)),
        ] + cast_out,
        out_shape=[
            jax.ShapeDtypeStruct((SEQ, D_MODEL), _BF16),
            jax.ShapeDtypeStruct((SEQ, N_BRANCHES * W), _BF16),
        ] + cast_shapes,
        scratch_shapes=[
            pltpu.VMEM((CONV_HALO + tm, W), _F32),
            pltpu.VMEM((POOL_HALO + tm, W), _F32),
            pltpu.VMEM((SCONV_HALO + tm, W), _F32),
        ],
        compiler_params=pltpu.CompilerParams(
            dimension_semantics=("arbitrary",),
            vmem_limit_bytes=V7X_VMEM_LIMIT_BYTES),
        name=f"mix_branches_l{layer}",
    )(x, ada, pre_g, mix_w_in, conv_w, conv_b, conv_ln_g, conv_ln_b, pool_group_w, pool_scale,
      sgu_ln_g, sgu_ln_b, sgu_w_s, sgu_b_s_t, sconv_w, *[job.src for job in cast_jobs])


def _mixb_kernel(n_cast, x_ref, n_ref, y_ref, ada_ref, post_g_ref, gw_ref, gb_ref, bw_ref, wo_ref,
                 *rest):
    cast_src, (o_ref, *cast_dst) = rest[:n_cast], rest[n_cast:]
    W = BRANCH_WIDTH
    j = pl.program_id(1)
    last = pl.num_programs(1) - 1

    def matmuls(rows, first):
        n = n_ref[rows, :]
        merged = None
        for b in range(N_BRANCHES):
            gate = _sigmoid(
                jnp.dot(n, gw_ref[b], preferred_element_type=_F32) + gb_ref[0, b:b + 1, :])
            term = gate * jnp.dot(y_ref[rows, b * W:(b + 1) * W], bw_ref[b],
                                  preferred_element_type=_F32)
            merged = term if merged is None else merged + term
        merged = merged.astype(_BF16)
        for c in range(D_MODEL // FFN_TN):
            cols = slice(c * FFN_TN, (c + 1) * FFN_TN)
            part = jnp.dot(merged, wo_ref[:, cols], preferred_element_type=_F32)
            if first:
                o_ref[rows, cols] = part
            else:
                o_ref[rows, cols] += part

    @pl.when(j == 0)
    def _():
        matmuls(slice(0, MIXB_TM), first=True)
        _run_casts(cast_src, cast_dst)

    @pl.when(jnp.logical_and(j > 0, j < last))
    def _():
        matmuls(slice(0, MIXB_TM), first=False)
        _run_casts(cast_src, cast_dst)

    @pl.when(j == last)
    def _():
        gain = post_g_ref[0, 1:2, :] * _mod(ada_ref, 1, 2)
        for block in _row_blocks(MIXB_TM, MIXB_LAST_BLOCKS):
            matmuls(block, first=False)
            _residual_epilogue(x_ref, o_ref, gain, block)
        _run_casts(cast_src, cast_dst)


def _mixb_call(x, n, y, ada, post_g, gate_w, gate_b, branch_w_out, w_o, layer, cast_jobs=()):
    tm, tn, W = MIXB_TM, MIXB_TN, BRANCH_WIDTH
    grid = (SEQ // tm, D_MODEL // tn)
    cast_in, cast_out, cast_shapes = _cast_specs(cast_jobs, grid)
    return pl.pallas_call(
        functools.partial(_mixb_kernel, len(cast_jobs)),
        grid=grid,
        in_specs=[
            pl.BlockSpec((tm, D_MODEL), lambda i, j: (i, 0)),
            pl.BlockSpec((tm, D_MODEL), lambda i, j: (i, 0)),
            pl.BlockSpec((tm, N_BRANCHES * W), lambda i, j: (i, 0)),
            pl.BlockSpec((1, 1, ADA_WIDTH), lambda i, j: (layer, 0, 0)),
            pl.BlockSpec((1, N_SUBLAYERS, D_MODEL), lambda i, j: (layer, 0, 0)),
            pl.BlockSpec((N_BRANCHES, D_MODEL, tn), lambda i, j: (0, 0, j)),
            pl.BlockSpec((1, N_BRANCHES, tn), lambda i, j: (layer, 0, j)),
            pl.BlockSpec((N_BRANCHES, W, tn), lambda i, j: (0, 0, j)),
            pl.BlockSpec((tn, D_MODEL), lambda i, j: (j, 0)),
        ] + cast_in,
        out_specs=[pl.BlockSpec((tm, D_MODEL), lambda i, j: (i, 0))] + cast_out,
        out_shape=[jax.ShapeDtypeStruct((SEQ, D_MODEL), _F32)] + cast_shapes,
        compiler_params=pltpu.CompilerParams(
            dimension_semantics=("arbitrary", "arbitrary"),
            vmem_limit_bytes=V7X_VMEM_LIMIT_BYTES),
        name=f"mix_merge_l{layer}",
    )(x, n, y, ada, post_g, gate_w, gate_b, branch_w_out, w_o, *[job.src for job in cast_jobs])


def kernel(x, c, ada_w, ada_b, pre_g, post_g, ffn_w_in, ffn_w_out, mix_w_in, gate_w, gate_b, conv_w, conv_b, conv_ln_g, conv_ln_b, pool_group_w, pool_scale, sgu_ln_g, sgu_ln_b, sgu_w_s, sgu_b_s, sconv_w, branch_w_out, w_o):
    assert x.shape == (1, SEQ, D_MODEL) and c.shape == (1, D_MODEL)
    L, W = DEPTH, BRANCH_WIDTH
    h = x.reshape(SEQ, D_MODEL)
    ada = _ada_call(c.reshape(D_MODEL, 1), ada_w, ada_b.reshape(L, 1, ADA_WIDTH))

    sgu_b_s_t = jnp.swapaxes(sgu_b_s, 1, 2)
    gate_w_rows = gate_w.reshape(L, N_BRANCHES * D_MODEL, D_MODEL)
    branch_w_rows = branch_w_out.reshape(L, N_BRANCHES * W, D_MODEL)
    ffn_steps = (SEQ // FFN_TM) * (D_FF // FFN_TF)
    mixa_steps = SEQ // MIXA_TM
    mixb_steps = (SEQ // MIXB_TM) * (D_MODEL // MIXB_TN)

    w_in_b = ffn_w_in[0, 0].astype(_BF16)
    w_out_b = ffn_w_out[0, 0].astype(_BF16)
    for l in range(DEPTH):
        h, mix_in_b, gate_b16 = _ffn_call(
            h, ada, pre_g, post_g, w_in_b, w_out_b, l, 0, 0, cast_jobs=(
                _cast_job(mix_w_in, (l,), ffn_steps), _cast_job(gate_w_rows, (l,), ffn_steps)))
        n, y, w_out_b, branch_b16, w_o_b = _mixa_call(
            h, ada, pre_g, mix_in_b, conv_w, conv_b, conv_ln_g, conv_ln_b,
            pool_group_w, pool_scale, sgu_ln_g, sgu_ln_b, sgu_w_s, sgu_b_s_t,
            sconv_w, l, cast_jobs=(
                _cast_job(ffn_w_out, (l, 1), mixa_steps), _cast_job(branch_w_rows, (l,), mixa_steps),
                _cast_job(w_o, (l,), mixa_steps)))
        h, w_in_b = _mixb_call(
            h, n, y, ada, post_g, gate_b16.reshape(N_BRANCHES, D_MODEL, D_MODEL), gate_b,
            branch_b16.reshape(N_BRANCHES, W, D_MODEL), w_o_b, l,
            cast_jobs=(_cast_job(ffn_w_in, (l, 1), mixb_steps),))
        next_jobs = () if l + 1 == DEPTH else (
            _cast_job(ffn_w_in, (l + 1, 0), ffn_steps), _cast_job(ffn_w_out, (l + 1, 0), ffn_steps))
        h, *next_w = _ffn_call(h, ada, pre_g, post_g, w_in_b, w_out_b, l, 1, 2, cast_jobs=next_jobs)
        if next_w:
            w_in_b, w_out_b = next_w
    return h.reshape(1, SEQ, D_MODEL)
```
